```python
import jax
import jax.numpy as jnp
from jax import lax
import numpy as np

D_MODEL = 2048
BATCH = 2
SEQ = 8192
DEPTH = 4

CTX_LEN = 256
GRID_W = 64
EPS = 1e-6

A_HEAD_DIM = 128
A_WIDTH = D_MODEL // 2
A_HEADS = A_WIDTH // A_HEAD_DIM
CHUNK = 128
ROWS_PER_CHUNK = CHUNK // GRID_W

B_HEADS = 4
B_V_WIDTH = D_MODEL - A_WIDTH
B_DV = B_V_WIDTH // B_HEADS
B_DK = B_DV // 2
B_K_WIDTH = B_HEADS * B_DK
DECAY_RANK = 16
DECAY_TAU = 16.0
GLA_CHUNK = 64

MIX_WIDTH = A_WIDTH + B_V_WIDTH
OFF_Q = 2 * A_WIDTH
OFF_G = OFF_Q + B_K_WIDTH
OFF_KVA = OFF_G + B_V_WIDTH
IN_WIDTH = OFF_KVA + B_K_WIDTH + B_V_WIDTH + 2 * DECAY_RANK

D_FF = 5632
N_EXPERTS = 8
TOP_K = 2
D_FF_EXPERT = 5632
MOE_BLOCK = 128
N_DENSE = (DEPTH + 1) // 2
N_MOE = DEPTH // 2

kernel_name = 'hybrid_gmlp_gla_moe_prefix_dit'


def rmsnorm(x, g):
    xf = x.astype(jnp.float32)
    y = xf * lax.rsqrt(jnp.mean(xf * xf, axis=-1, keepdims=True) + EPS)
    return (y * g.astype(jnp.float32)).astype(x.dtype)


def modulate(h, shift, scale):
    return h * (1 + scale) + shift


def flip(t):
    return t[:, ::-1]


def chunk_gmlp(u, v, g_v, w_s, b_s, n_chunks):
    bsz, L, _ = u.shape
    u = jax.nn.gelu(u, approximate=False)
    v = rmsnorm(jax.nn.gelu(v, approximate=False), g_v)
    vh = v.reshape(bsz, n_chunks, CHUNK, A_HEADS, A_HEAD_DIM)
    s = jnp.einsum('hpq,bcqhd->bcphd', w_s, vh) + b_s.T[None, None, :, :, None]
    return (u.reshape(vh.shape) * s).reshape(bsz, L, A_WIDTH)


def gla_kva(p_kva, w_dec, b_dec):
    bsz, L, _ = p_kva.shape
    k = p_kva[..., :B_K_WIDTH].reshape(bsz, L, B_HEADS, B_DK)
    v = p_kva[..., B_K_WIDTH:B_K_WIDTH + B_V_WIDTH].reshape(bsz, L, B_HEADS, B_DV)
    a = p_kva[..., B_K_WIDTH + B_V_WIDTH:].reshape(bsz, L, 2, DECAY_RANK)
    z = jnp.einsum('blnr,nrk->blnk', a, w_dec) + b_dec
    log_a = (jax.nn.log_sigmoid(z.astype(jnp.float32)) / DECAY_TAU).reshape(bsz, L, 2, B_HEADS, B_DK)
    return k, v, log_a[:, :, 0], log_a[:, :, 1]


def gla_direction(q, k, v, log_a, s0):
    bsz, L, H, _ = q.shape
    n = L // GLA_CHUNK

    def to_chunks(t):
        return t.astype(jnp.float32).reshape(bsz, n, GLA_CHUNK, H, t.shape[-1]).transpose(1, 0, 3, 2, 4)

    tri = jnp.tril(jnp.ones((GLA_CHUNK, GLA_CHUNK), bool))

    def step(S, inp):
        qc, kc, vc, ac = inp
        b = jnp.cumsum(ac, axis=2)
        diff = b[:, :, :, None, :] - b[:, :, None, :, :]
        decay = jnp.exp(jnp.where(tri[:, :, None], diff, -jnp.inf))
        scores = jnp.einsum('bhijd,bhjd->bhij', decay * qc[:, :, :, None, :], kc)
        o = jnp.einsum('bhij,bhjv->bhiv', scores, vc) + jnp.einsum('bhid,bhdv->bhiv', qc * jnp.exp(b), S)
        b_end = b[:, :, -1:, :]
        S = jnp.exp(b_end[:, :, 0, :, None]) * S + jnp.einsum('bhjd,bhjv->bhdv', kc * jnp.exp(b_end - b), vc)
        return S, o

    S, o = lax.scan(step, s0, (to_chunks(q), to_chunks(k), to_chunks(v), to_chunks(log_a)))
    o = o.transpose(1, 0, 3, 2, 4).reshape(bsz, L, H, -1)
    return o, S


def gla_state(k, v, log_a):
    b = jnp.cumsum(log_a.astype(jnp.float32), axis=1)
    w = jnp.exp(b[:, -1:] - b)
    return jnp.einsum('blhd,blhv->bhdv', k.astype(jnp.float32) * w, v.astype(jnp.float32))


def gla_bidir(q, k, v, la_f, la_b, s_f, s_b):
    o_f, s_f = gla_direction(q, k, v, la_f, s_f)
    o_b, s_b = gla_direction(flip(q), flip(k), flip(v), flip(la_b), s_b)
    return o_f + flip(o_b), s_f, s_b


def gla_out(o, g, g_norm):
    bsz, L = o.shape[:2]
    o = o * lax.rsqrt(jnp.mean(o * o, axis=-1, keepdims=True) + EPS)
    o = o.reshape(bsz, L, B_V_WIDTH) * g_norm.astype(jnp.float32)
    return (o * jax.nn.silu(g.astype(jnp.float32))).astype(g.dtype)


def mix_stream(p, n_chunks, s_f, s_b, g_v, w_s, b_s, w_dec, b_dec, g_gla):
    bsz, L, _ = p.shape
    y_a = chunk_gmlp(p[..., :A_WIDTH], p[..., A_WIDTH:OFF_Q], g_v, w_s, b_s, n_chunks)
    q = p[..., OFF_Q:OFF_G].reshape(bsz, L, B_HEADS, B_DK) * (B_DK ** -0.5)
    g = p[..., OFF_G:OFF_KVA]
    k, v, la_f, la_b = gla_kva(p[..., OFF_KVA:], w_dec, b_dec)
    o, s_f, s_b = gla_bidir(q, k, v, la_f, la_b, s_f, s_b)
    y_b = gla_out(o, g, g_gla)
    return jnp.concatenate([y_a, y_b], axis=-1), s_f, s_b


def swiglu(h, w_g, w_u, w_d):
    return (jax.nn.silu(h @ w_g) * (h @ w_u)) @ w_d


def moe_swiglu(h, w_router, w_gate, w_up, w_down):
    lead = h.shape[:-1]
    x2 = h.reshape(-1, h.shape[-1])
    n = x2.shape[0]
    logits = (x2 @ w_router).astype(jnp.float32)
    top_val, top_idx = lax.top_k(logits, TOP_K)
    top_w = jax.nn.softmax(top_val, axis=-1)
    flat_e = top_idx.reshape(-1)
    flat_tok = jnp.repeat(jnp.arange(n, dtype=jnp.int32), TOP_K)
    flat_w = top_w.reshape(-1)
    order = jnp.argsort(flat_e)
    sorted_e = flat_e[order]
    counts = jnp.bincount(flat_e, length=N_EXPERTS)
    padded = (counts + MOE_BLOCK - 1) // MOE_BLOCK * MOE_BLOCK
    start = jnp.cumsum(counts) - counts
    pad_end = jnp.cumsum(padded)
    pad_start = pad_end - padded
    rank = jnp.arange(n * TOP_K, dtype=jnp.int32) - start[sorted_e]
    dest = pad_start[sorted_e] + rank
    n_blocks = -(-(n * TOP_K + N_EXPERTS * (MOE_BLOCK - 1)) // MOE_BLOCK)
    cap = n_blocks * MOE_BLOCK
    buf_tok = jnp.zeros((cap,), jnp.int32).at[dest].set(flat_tok[order])
    buf_w = jnp.zeros((cap,), jnp.float32).at[dest].set(flat_w[order])
    block_e = jnp.minimum(jnp.searchsorted(pad_end, jnp.arange(n_blocks, dtype=jnp.int32) * MOE_BLOCK, side='right'), N_EXPERTS - 1)

    def expert_block(args):
        tok, e = args
        xb = x2[tok]
        return (jax.nn.silu(xb @ w_gate[e]) * (xb @ w_up[e])) @ w_down[e]

    out = lax.map(expert_block, (buf_tok.reshape(n_blocks, MOE_BLOCK), block_e))
    contrib = (out.reshape(cap, -1).astype(jnp.float32) * buf_w[:, None]).astype(h.dtype)
    y = jnp.zeros_like(x2).at[buf_tok].add(contrib)
    return y.reshape(*lead, -1)


def setup_inputs(seed: int = 0) -> dict:
    key = jax.random.key(seed)
    ks = iter(jax.random.split(key, 32))
    D = D_MODEL

    def nrm(shape, scale):
        return jax.random.normal(next(ks), shape, jnp.float32) * scale

    return {
        'x': nrm((BATCH, SEQ, D), 1.0),
        'c': nrm((BATCH, D), 1.0),
        'ctx': nrm((BATCH, CTX_LEN, D), 1.0),
        'c_ctx': nrm((D,), 1.0),
        'w_mod': nrm((DEPTH, D, 6 * D), 0.5 * D ** -0.5),
        'b_mod': nrm((DEPTH, 6 * D), 0.02),
        'g_pre_mix': 1.0 + nrm((DEPTH, D), 0.1),
        'g_post_mix': 1.0 + nrm((DEPTH, D), 0.1),
        'g_pre_ffn': 1.0 + nrm((DEPTH, D), 0.1),
        'g_post_ffn': 1.0 + nrm((DEPTH, D), 0.1),
        'w_in': nrm((DEPTH, D, IN_WIDTH), D ** -0.5),
        'w_out': nrm((DEPTH, MIX_WIDTH, D), MIX_WIDTH ** -0.5),
        'g_vnorm': 1.0 + nrm((DEPTH, A_WIDTH), 0.1),
        'w_spatial': nrm((DEPTH, A_HEADS, CHUNK, CHUNK), CHUNK ** -0.5),
        'b_spatial': 1.0 + nrm((DEPTH, A_HEADS, CHUNK), 0.1),
        'w_decay': nrm((DEPTH, 2, DECAY_RANK, B_K_WIDTH), DECAY_RANK ** -0.5),
        'b_decay': 1.0 + nrm((DEPTH, 2, B_K_WIDTH), 0.5),
        'g_gla_norm': 1.0 + nrm((DEPTH, B_V_WIDTH), 0.1),
        'w_ffn_gate': nrm((N_DENSE, D, D_FF), D ** -0.5),
        'w_ffn_up': nrm((N_DENSE, D, D_FF), D ** -0.5),
        'w_ffn_down': nrm((N_DENSE, D_FF, D), D_FF ** -0.5),
        'w_router': nrm((N_MOE, D, N_EXPERTS), D ** -0.5),
        'w_exp_gate': nrm((N_MOE, N_EXPERTS, D, D_FF_EXPERT), D ** -0.5),
        'w_exp_up': nrm((N_MOE, N_EXPERTS, D, D_FF_EXPERT), D ** -0.5),
        'w_exp_down': nrm((N_MOE, N_EXPERTS, D_FF_EXPERT, D), D_FF_EXPERT ** -0.5),
    }


def reference(x, c, ctx, c_ctx, w_mod, b_mod, g_pre_mix, g_post_mix, g_pre_ffn, g_post_ffn,
              w_in, w_out, g_vnorm, w_spatial, b_spatial, w_decay, b_decay, g_gla_norm,
              w_ffn_gate, w_ffn_up, w_ffn_down, w_router, w_exp_gate, w_exp_up, w_exp_down):
    bsz, seq, _ = x.shape
    ctx_len = ctx.shape[1]
    rows = seq // GRID_W
    n_lat_chunks = rows // ROWS_PER_CHUNK
    n_ctx_chunks = ctx_len // CHUNK
    zero_state = jnp.zeros((bsz, B_HEADS, B_DK, B_DV), jnp.float32)
    xc = ctx
    for l in range(DEPTH):
        last = l == DEPTH - 1
        mod_x = (jax.nn.silu(c) @ w_mod[l] + b_mod[l])[:, None, :]
        mod_c = jax.nn.silu(c_ctx) @ w_mod[l] + b_mod[l]
        sh_m, sc_m, gt_m, sh_f, sc_f, gt_f = jnp.split(mod_x, 6, axis=-1)
        csh_m, csc_m, cgt_m, csh_f, csc_f, cgt_f = jnp.split(mod_c, 6, axis=-1)
        mp = (g_vnorm[l], w_spatial[l], b_spatial[l], w_decay[l], b_decay[l], g_gla_norm[l])

        hx = modulate(rmsnorm(x, g_pre_mix[l]), sh_m, sc_m)
        hc = modulate(rmsnorm(xc, g_pre_mix[l]), csh_m, csc_m)
        if last:
            k_c, v_c, la_f_c, la_b_c = gla_kva(hc @ w_in[l][:, OFF_KVA:], w_decay[l], b_decay[l])
            s_f = gla_state(k_c, v_c, la_f_c)
            s_b = gla_state(flip(k_c), flip(v_c), flip(la_b_c))
        else:
            y_c, s_f, s_b = mix_stream(hc @ w_in[l], n_ctx_chunks, zero_state, zero_state, *mp)
            xc = xc + cgt_m * rmsnorm(y_c @ w_out[l], g_post_mix[l])
        y_x, _, _ = mix_stream(hx @ w_in[l], n_lat_chunks, s_f, s_b, *mp)
        x = x + gt_m * rmsnorm(y_x @ w_out[l], g_post_mix[l])

        fx = modulate(rmsnorm(x, g_pre_ffn[l]), sh_f, sc_f)
        if last:
            f_in = fx
        else:
            fc = modulate(rmsnorm(xc, g_pre_ffn[l]), csh_f, csc_f)
            f_in = jnp.concatenate([jnp.broadcast_to(fc, (bsz,) + fc.shape[1:]), fx], axis=1)
        if l % 2 == 0:
            f_out = swiglu(f_in, w_ffn_gate[l // 2], w_ffn_up[l // 2], w_ffn_down[l // 2])
        else:
            f_out = moe_swiglu(f_in, w_router[l // 2], w_exp_gate[l // 2], w_exp_up[l // 2], w_exp_down[l // 2])
        if last:
            f_x = f_out
        else:
            f_x = f_out[:, ctx_len:]
            xc = xc + cgt_f * rmsnorm(f_out[:, :ctx_len], g_post_ffn[l])
        x = x + gt_f * rmsnorm(f_x, g_post_ffn[l])
    return x
```

```python
import functools

import jax
import jax.numpy as jnp
from jax import lax
from jax.experimental import pallas as pl
from jax.experimental.pallas import tpu as pltpu

_F32 = jnp.float32
_BF16 = jnp.bfloat16

EPS = 1e-6
CHUNK = 128
MOD_BLOCK = 256
A_HEADS = 8
A_HEAD_DIM = 128
B_HEADS = 4
B_DK = 128
B_DV = 256
DECAY_RANK = 16
DECAY_TAU = 16.0
N_EXPERTS = 8
ROUTE_ROWS = 8
TOKEN_TILE = 512
FFN_TOKEN_TILE = 768
FF_TILE = 512
EXPERT_ROWS = 512
COMBINE_ROWS = 256
VMEM_LIMIT = 56 << 20


def _dot(a, b):
    return jnp.dot(a, b, preferred_element_type=_F32)


def _dot_nt(a, b):
    return lax.dot_general(a, b, (((1,), (1,)), ((), ())), preferred_element_type=_F32)


def _dot_tn(a, b):
    return lax.dot_general(a, b, (((0,), (0,)), ((), ())), preferred_element_type=_F32)


def _split(x):
    hi = x.astype(_BF16)
    lo = (x - hi.astype(_F32)).astype(_BF16)
    return hi, lo


def _dot_f32(a, b, dot=_dot):
    ah, al = _split(a)
    bh, bl = _split(b)
    return dot(ah, bh) + dot(ah, bl) + dot(al, bh)


def _rms(x, g):
    return x * lax.rsqrt(jnp.mean(x * x, axis=-1, keepdims=True) + EPS) * g


def _gelu(x):
    return 0.5 * x * (1.0 + lax.erf(x * (2.0 ** -0.5)))


def _silu(x):
    return x * jax.nn.sigmoid(x)


def _log_sigmoid(z):
    return jnp.minimum(z, 0.0) - jnp.log1p(jnp.exp(-jnp.abs(z)))


def _params(semantics):
    return pltpu.CompilerParams(dimension_semantics=semantics, vmem_limit_bytes=VMEM_LIMIT)


def _resident(shape):
    zeros = (0,) * len(shape)
    return pl.BlockSpec(shape, lambda *_: zeros, pipeline_mode=pl.Buffered(1))


def _mod_kernel(c_ref, w_ref, b_ref, o_ref):
    s = _silu(c_ref[...]).astype(_BF16)
    o_ref[0] = _dot(s, w_ref[0].astype(_BF16)) + b_ref[0]


def _mod_call(cvec, w_mod, b_mod):
    depth, d, n = w_mod.shape
    tn = 1024
    return pl.pallas_call(
        _mod_kernel,
        grid=(depth, n // tn),
        in_specs=[
            pl.BlockSpec((ROUTE_ROWS, d), lambda l, j: (0, 0)),
            pl.BlockSpec((1, d, tn), lambda l, j: (l, 0, j)),
            pl.BlockSpec((1, 1, tn), lambda l, j: (l, 0, j)),
        ],
        out_specs=pl.BlockSpec((1, ROUTE_ROWS, tn), lambda l, j: (l, 0, j)),
        out_shape=jax.ShapeDtypeStruct((depth, ROUTE_ROWS, n), _F32),
        compiler_params=_params(("parallel", "parallel")),
        name="adaln_mod",
    )(cvec, w_mod, b_mod.reshape(depth, 1, n))


def _inproj_kernel(x_ref, mt_ref, g_ref, w_ref, wa_ref,
                   uv_ref, q_ref, gate_ref, k_ref, v_ref, a_ref, h_scr):
    tm = x_ref.shape[0]
    for s in range(tm // MOD_BLOCK):
        rows = slice(s * MOD_BLOCK, (s + 1) * MOD_BLOCK)
        mt = mt_ref[s]
        h = _rms(x_ref[rows, :], g_ref[...]) * (1.0 + mt[1:2]) + mt[0:1]
        h_scr[rows, :] = h.astype(_BF16)
    h = h_scr[...]
    off = 0
    for out_ref in (uv_ref, q_ref, gate_ref, k_ref, v_ref):
        width = out_ref.shape[1]
        for c0 in range(0, width, 1024):
            cw = min(1024, width - c0)
            out_ref[:, c0:c0 + cw] = _dot(h, w_ref[:, off + c0:off + c0 + cw]).astype(_BF16)
        off += width
    a_ref[...] = _dot(h, wa_ref[...])


def _inproj_call(x, modtab, g_pre, w_main, w_a):
    n, d = x.shape
    tm = TOKEN_TILE
    a_w = A_HEADS * A_HEAD_DIM
    k_w = B_HEADS * B_DK
    v_w = B_HEADS * B_DV
    widths = (2 * a_w, k_w, v_w, k_w, v_w)
    tok = lambda w: pl.BlockSpec((tm, w), lambda i: (i, 0))
    return pl.pallas_call(
        _inproj_kernel,
        grid=(n // tm,),
        in_specs=[
            tok(d),
            pl.BlockSpec((tm // MOD_BLOCK, 8, d), lambda i: (i, 0, 0)),
            pl.BlockSpec((1, d), lambda i: (0, 0)),
            _resident(w_main.shape),
            _resident(w_a.shape),
        ],
        out_specs=[tok(w) for w in widths] + [tok(2 * CHUNK)],
        out_shape=[jax.ShapeDtypeStruct((n, w), _BF16) for w in widths]
        + [jax.ShapeDtypeStruct((n, 2 * CHUNK), _F32)],
        scratch_shapes=[pltpu.VMEM((tm, d), _BF16)],
        compiler_params=_params(("parallel",)),
        name="in_proj",
    )(x, modtab, g_pre, w_main, w_a)


def _gmlp_kernel(uv_ref, gv_ref, ws_ref, bs_ref, y_ref):
    tm = uv_ref.shape[0]
    a_w = A_HEADS * A_HEAD_DIM
    for c in range(tm // CHUNK):
        rows = slice(c * CHUNK, (c + 1) * CHUNK)
        u = _gelu(uv_ref[rows, :a_w].astype(_F32))
        v = _rms(_gelu(uv_ref[rows, a_w:].astype(_F32)), gv_ref[...]).astype(_BF16)
        for h in range(A_HEADS):
            cols = slice(h * A_HEAD_DIM, (h + 1) * A_HEAD_DIM)
            s = _dot(ws_ref[h], v[:, cols]) + bs_ref[h]
            y_ref[rows, cols] = (u[:, cols] * s).astype(_BF16)


def _gmlp_call(uv, g_v, w_s, b_s):
    n = uv.shape[0]
    tm = TOKEN_TILE
    a_w = A_HEADS * A_HEAD_DIM
    return pl.pallas_call(
        _gmlp_kernel,
        grid=(n // tm,),
        in_specs=[
            pl.BlockSpec((tm, 2 * a_w), lambda i: (i, 0)),
            pl.BlockSpec((1, a_w), lambda i: (0, 0)),
            pl.BlockSpec((A_HEADS, CHUNK, CHUNK), lambda i: (0, 0, 0)),
            pl.BlockSpec((A_HEADS, CHUNK, 1), lambda i: (0, 0, 0)),
        ],
        out_specs=pl.BlockSpec((tm, a_w), lambda i: (i, 0)),
        out_shape=jax.ShapeDtypeStruct((n, a_w), _BF16),
        compiler_params=_params(("parallel",)),
        name="gmlp",
    )(uv, g_v, w_s, b_s)


def _gla_kernel(qf_ref, kf_ref, vf_ref, af_ref, qb_ref, kb_ref, vb_ref, ab_ref,
                wd_ref, bd_ref, wdt_ref, bdt_ref, of_ref, ob_ref, s_ref):
    @pl.when(pl.program_id(1) == 0)
    def _():
        s_ref[...] = jnp.zeros_like(s_ref)

    row = lax.broadcasted_iota(jnp.int32, (CHUNK, CHUNK), 0)
    col = lax.broadcasted_iota(jnp.int32, (CHUNK, CHUNK), 1)
    inv_tau = 1.0 / DECAY_TAU
    streams = ((qf_ref, kf_ref, vf_ref, af_ref, of_ref), (qb_ref, kb_ref, vb_ref, ab_ref, ob_ref))
    for d, (q_ref, k_ref, v_ref, a_ref, o_ref) in enumerate(streams):
        causal = (col <= row) if d == 0 else (col >= row)
        tri = jnp.where(causal, 1.0, 0.0).astype(_BF16)
        end = CHUNK - 1 if d == 0 else 0
        a = a_ref[:, d * CHUNK:(d + 1) * CHUNK]
        la = _log_sigmoid(_dot_f32(a, wd_ref[d]) + bd_ref[d]) * inv_tau
        la_hi, la_lo = _split(la)
        b = _dot(tri, la_hi) + _dot(tri, la_lo)
        b_mid = b[CHUNK // 2:CHUNK // 2 + 1]
        b_end = b[end:end + 1]
        qd = q_ref[...].astype(_F32) * (B_DK ** -0.5) * jnp.exp(b - b_mid)
        kd = k_ref[...].astype(_F32) * jnp.exp(b_mid - b)
        qi = (qd * jnp.exp(b_mid)).astype(_BF16)
        ks = (kd * jnp.exp(b_end - b_mid)).astype(_BF16)
        qd = qd.astype(_BF16)
        kd = kd.astype(_BF16)
        la_t = _log_sigmoid(_dot_f32(wdt_ref[d], a, _dot_nt) + bdt_ref[d]) * inv_tau
        dcol = jnp.exp(jnp.sum(la_t, axis=1, keepdims=True))
        for h in range(B_HEADS):
            kc = slice(h * B_DK, (h + 1) * B_DK)
            vc = slice(h * B_DV, (h + 1) * B_DV)
            vh = v_ref[:, vc]
            state = s_ref[d, h]
            scores = jnp.where(causal, _dot_nt(qd[:, kc], kd[:, kc]), 0.0).astype(_BF16)
            o_ref[:, vc] = _dot(scores, vh) + _dot(qi[:, kc], state.astype(_BF16))
            s_ref[d, h] = dcol[kc] * state + _dot_tn(ks[:, kc], vh)


def _gla_call(q, k, v, a, wd, bd, wdt, bdt, n_batch, ctx_chunks, lat_chunks):
    n = q.shape[0]
    k_w = B_HEADS * B_DK
    v_w = B_HEADS * B_DV
    steps = ctx_chunks + lat_chunks
    lat0 = n_batch * ctx_chunks

    def fwd(b, t):
        return jnp.where(t < ctx_chunks, b * ctx_chunks + t, lat0 + b * lat_chunks + (t - ctx_chunks))

    def bwd(b, t):
        return jnp.where(t < ctx_chunks, b * ctx_chunks + (ctx_chunks - 1 - t),
                         lat0 + b * lat_chunks + (steps - 1 - t))

    def specs(chunk_of):
        tok = lambda w: pl.BlockSpec((CHUNK, w), lambda b, t: (chunk_of(b, t), 0))
        return [tok(k_w), tok(k_w), tok(v_w), tok(2 * CHUNK)]

    full = lambda arr: pl.BlockSpec(arr.shape, lambda b, t: (0,) * arr.ndim)
    return pl.pallas_call(
        _gla_kernel,
        grid=(n_batch, steps),
        in_specs=specs(fwd) + specs(bwd) + [full(wd), full(bd), full(wdt), full(bdt)],
        out_specs=[pl.BlockSpec((CHUNK, v_w), lambda b, t: (fwd(b, t), 0)),
                   pl.BlockSpec((CHUNK, v_w), lambda b, t: (bwd(b, t), 0))],
        out_shape=[jax.ShapeDtypeStruct((n, v_w), _F32)] * 2,
        scratch_shapes=[pltpu.VMEM((2, B_HEADS, B_DK, B_DV), _F32)],
        compiler_params=_params(("parallel", "arbitrary")),
        name="gla_scan",
    )(q, k, v, a, q, k, v, a, wd, bd, wdt, bdt)


def _outproj_kernel(*refs, routed):
    if routed:
        (ya_ref, of_ref, ob_ref, gate_ref, x_ref, mt_ref, gn_ref, gpost_ref, gffn_ref, w_ref, wr_ref,
         xo_ref, f_ref, route_ref) = refs
    else:
        (ya_ref, of_ref, ob_ref, gate_ref, x_ref, mt_ref, gn_ref, gpost_ref, gffn_ref, w_ref,
         xo_ref, f_ref) = refs
    a_w = A_HEADS * A_HEAD_DIM
    tm = x_ref.shape[0]
    m = _dot(ya_ref[...], w_ref[:a_w, :])
    for h in range(B_HEADS):
        vc = slice(h * B_DV, (h + 1) * B_DV)
        o = of_ref[:, vc] + ob_ref[:, vc]
        o = o * lax.rsqrt(jnp.mean(o * o, axis=-1, keepdims=True) + EPS) * gn_ref[:, vc]
        yb = (o * _silu(gate_ref[:, vc].astype(_F32))).astype(_BF16)
        m = m + _dot(yb, w_ref[a_w + h * B_DV:a_w + (h + 1) * B_DV, :])
    m = _rms(m, gpost_ref[...])
    for s in range(tm // MOD_BLOCK):
        rows = slice(s * MOD_BLOCK, (s + 1) * MOD_BLOCK)
        mt = mt_ref[s]
        x = x_ref[rows, :] + mt[2:3] * m[rows, :]
        xo_ref[rows, :] = x
        f = _rms(x, gffn_ref[...]) * (1.0 + mt[4:5]) + mt[3:4]
        f_ref[rows, :] = f.astype(f_ref.dtype)
    if routed:
        logits = _dot_f32(wr_ref[...], f_ref[...], _dot_nt)
        eidx = lax.broadcasted_iota(jnp.int32, logits.shape, 0).astype(_F32)
        none = float(N_EXPERTS)
        m1 = jnp.max(logits, axis=0, keepdims=True)
        i1 = jnp.min(jnp.where(logits == m1, eidx, none), axis=0, keepdims=True)
        rest = jnp.where(eidx == i1, -jnp.inf, logits)
        m2 = jnp.max(rest, axis=0, keepdims=True)
        i2 = jnp.min(jnp.where(rest == m2, eidx, none), axis=0, keepdims=True)
        e2 = jnp.exp(m2 - m1)
        w1 = 1.0 / (1.0 + e2)
        w2 = e2 / (1.0 + e2)
        r = lax.broadcasted_iota(jnp.int32, logits.shape, 0)
        route_ref[...] = jnp.where(r == 0, i1, jnp.where(r == 1, i2, jnp.where(r == 2, w1, jnp.where(r == 3, w2, 0.0))))


def _outproj_call(ya, o_f, o_b, gate, x, modtab, g_norm, g_post, g_ffn, w_out, wr_t):
    n, d = x.shape
    tm = TOKEN_TILE
    routed = wr_t is not None
    tok = lambda w: pl.BlockSpec((tm, w), lambda i: (i, 0))
    vec = lambda w: pl.BlockSpec((1, w), lambda i: (0, 0))
    a_w = A_HEADS * A_HEAD_DIM
    v_w = B_HEADS * B_DV
    in_specs = [tok(a_w), tok(v_w), tok(v_w), tok(v_w), tok(d),
                pl.BlockSpec((tm // MOD_BLOCK, 8, d), lambda i: (i, 0, 0)),
                vec(v_w), vec(d), vec(d), _resident(w_out.shape)]
    args = [ya, o_f, o_b, gate, x, modtab, g_norm, g_post, g_ffn, w_out]
    out_specs = [tok(d), tok(d)]
    out_shape = [jax.ShapeDtypeStruct((n, d), _F32), jax.ShapeDtypeStruct((n, d), _F32 if routed else _BF16)]
    if routed:
        in_specs.append(pl.BlockSpec(wr_t.shape, lambda i: (0, 0)))
        args.append(wr_t)
        out_specs.append(pl.BlockSpec((ROUTE_ROWS, tm), lambda i: (0, i)))
        out_shape.append(jax.ShapeDtypeStruct((ROUTE_ROWS, n), _F32))
    return pl.pallas_call(
        functools.partial(_outproj_kernel, routed=routed),
        grid=(n // tm,),
        in_specs=in_specs,
        out_specs=out_specs,
        out_shape=out_shape,
        compiler_params=_params(("parallel",)),
        name="out_proj_routed" if routed else "out_proj",
    )(*args)


def _ffn_kernel(f_ref, wg_ref, wu_ref, wd_ref, x_ref, mt_ref, g_ref, o_ref):
    j = pl.program_id(1)

    @pl.when(j == 0)
    def _():
        o_ref[...] = jnp.zeros_like(o_ref)

    f = f_ref[...]
    act = (_silu(_dot(f, wg_ref[...])) * _dot(f, wu_ref[...])).astype(_BF16)
    o_ref[...] += _dot(act, wd_ref[...])

    @pl.when(j == pl.num_programs(1) - 1)
    def _():
        for s in range(o_ref.shape[0] // MOD_BLOCK):
            rows = slice(s * MOD_BLOCK, (s + 1) * MOD_BLOCK)
            o_ref[rows, :] = x_ref[rows, :] + mt_ref[s][5:6] * _rms(o_ref[rows, :], g_ref[...])


def _token_tile(n, target):
    tm = target - target % MOD_BLOCK
    while n % tm:
        tm -= MOD_BLOCK
    return tm


def _ffn_call(f, w_gate, w_up, w_down, x, modtab, g_post):
    n, d = x.shape
    d_ff = w_gate.shape[1]
    tm = _token_tile(n, FFN_TOKEN_TILE)
    tf = FF_TILE
    return pl.pallas_call(
        _ffn_kernel,
        grid=(n // tm, d_ff // tf),
        in_specs=[
            pl.BlockSpec((tm, d), lambda i, j: (i, 0)),
            pl.BlockSpec((d, tf), lambda i, j: (0, j)),
            pl.BlockSpec((d, tf), lambda i, j: (0, j)),
            pl.BlockSpec((tf, d), lambda i, j: (j, 0)),
            pl.BlockSpec((tm, d), lambda i, j: (i, 0)),
            pl.BlockSpec((tm // MOD_BLOCK, 8, d), lambda i, j: (i, 0, 0)),
            pl.BlockSpec((1, d), lambda i, j: (0, 0)),
        ],
        out_specs=pl.BlockSpec((tm, d), lambda i, j: (i, 0)),
        out_shape=jax.ShapeDtypeStruct((n, d), _F32),
        compiler_params=_params(("parallel", "arbitrary")),
        name="dense_ffn",
    )(f, w_gate, w_up, w_down, x, modtab, g_post)


def _rank_kernel(route_ref, rank_ref, count_ref, carry):
    @pl.when(pl.program_id(0) == 0)
    def _():
        carry[...] = jnp.zeros_like(carry)

    tl = route_ref.shape[1]
    r = route_ref[...]
    eidx = lax.broadcasted_iota(jnp.int32, r.shape, 0).astype(_F32)
    oh1 = jnp.where(eidx == r[0:1], 1.0, 0.0)
    oh2 = jnp.where(eidx == r[1:2], 1.0, 0.0)
    both = oh1 + oh2
    before = lax.broadcasted_iota(jnp.int32, (tl, tl), 0) < lax.broadcasted_iota(jnp.int32, (tl, tl), 1)
    seen = carry[:, 0:1] + _dot(both.astype(_BF16), jnp.where(before, 1.0, 0.0).astype(_BF16))
    rank1 = jnp.sum(oh1 * seen, axis=0, keepdims=True)
    rank2 = jnp.sum(oh2 * seen, axis=0, keepdims=True)
    row = lax.broadcasted_iota(jnp.int32, r.shape, 0)
    rank_ref[...] = jnp.where(row == 0, rank1, jnp.where(row == 1, rank2, 0.0))
    carry[...] = carry[...] + jnp.sum(both, axis=1, keepdims=True)
    count_ref[...] = carry[...]


def _rank_call(route):
    n = route.shape[1]
    tl = TOKEN_TILE
    return pl.pallas_call(
        _rank_kernel,
        grid=(n // tl,),
        in_specs=[pl.BlockSpec((ROUTE_ROWS, tl), lambda i: (0, i))],
        out_specs=[pl.BlockSpec((ROUTE_ROWS, tl), lambda i: (0, i)),
                   pl.BlockSpec((ROUTE_ROWS, 128), lambda i: (0, 0))],
        out_shape=[jax.ShapeDtypeStruct((ROUTE_ROWS, n), _F32),
                   jax.ShapeDtypeStruct((ROUTE_ROWS, 128), _F32)],
        scratch_shapes=[pltpu.VMEM((ROUTE_ROWS, 128), _F32)],
        compiler_params=_params(("arbitrary",)),
        name="moe_rank",
    )(route)


def _dispatch_kernel(dest_ref, f_ref, xs_in_ref, xs_ref, sem):
    del xs_in_ref
    td = dest_ref.shape[1] // 2
    base = pl.program_id(0) * td

    def row_copy(t, k):
        return pltpu.make_async_copy(f_ref.at[pl.ds(base + t, 1)], xs_ref.at[pl.ds(dest_ref[0, k * td + t], 1)], sem)

    def issue(t, carry):
        row_copy(t, 0).start()
        row_copy(t, 1).start()
        return carry

    def drain(t, carry):
        row_copy(t, 0).wait()
        row_copy(t, 1).wait()
        return carry

    lax.fori_loop(0, td, issue, 0)
    lax.fori_loop(0, td, drain, 0)


def _dispatch_call(dest_blocks, f, xs_init):
    n_steps = dest_blocks.shape[0]
    return pl.pallas_call(
        _dispatch_kernel,
        grid=(n_steps,),
        in_specs=[
            pl.BlockSpec((None, 1, dest_blocks.shape[2]), lambda i: (i, 0, 0), memory_space=pltpu.SMEM),
            pl.BlockSpec(memory_space=pl.ANY),
            pl.BlockSpec(memory_space=pl.ANY),
        ],
        out_specs=pl.BlockSpec(memory_space=pl.ANY),
        out_shape=jax.ShapeDtypeStruct(xs_init.shape, xs_init.dtype),
        scratch_shapes=[pltpu.SemaphoreType.DMA(())],
        input_output_aliases={2: 0},
        compiler_params=_params(("arbitrary",)),
        name="moe_dispatch",
    )(dest_blocks, f, xs_init)


def _expert_kernel(be_ref, nv_ref, xs_ref, wg_ref, wu_ref, wd_ref, ys_ref, xb_scr):
    del be_ref
    i = pl.program_id(0)
    j = pl.program_id(1)

    @pl.when(j == 0)
    def _():
        ys_ref[...] = jnp.zeros_like(ys_ref)
        xb_scr[...] = xs_ref[...].astype(_BF16)

    @pl.when(i < nv_ref[0])
    def _():
        xb = xb_scr[...]
        act = (_silu(_dot(xb, wg_ref[...])) * _dot(xb, wu_ref[...])).astype(_BF16)
        ys_ref[...] += _dot(act, wd_ref[...])


def _expert_call(block_expert, n_valid, xs, w_gate, w_up, w_down):
    cap, d = xs.shape
    d_ff = w_gate.shape[2]
    tb = EXPERT_ROWS
    tf = FF_TILE
    n_ff = d_ff // tf

    def ff_of(i, j, nv):
        return jnp.where(i < nv[0], j, n_ff - 1)

    grid_spec = pltpu.PrefetchScalarGridSpec(
        num_scalar_prefetch=2,
        grid=(cap // tb, n_ff),
        in_specs=[
            pl.BlockSpec((tb, d), lambda i, j, be, nv: (i, 0)),
            pl.BlockSpec((None, d, tf), lambda i, j, be, nv: (be[i], 0, ff_of(i, j, nv))),
            pl.BlockSpec((None, d, tf), lambda i, j, be, nv: (be[i], 0, ff_of(i, j, nv))),
            pl.BlockSpec((None, tf, d), lambda i, j, be, nv: (be[i], ff_of(i, j, nv), 0)),
        ],
        out_specs=pl.BlockSpec((tb, d), lambda i, j, be, nv: (i, 0)),
        scratch_shapes=[pltpu.VMEM((tb, d), _BF16)],
    )
    return pl.pallas_call(
        _expert_kernel,
        grid_spec=grid_spec,
        out_shape=jax.ShapeDtypeStruct((cap, d), _F32),
        compiler_params=_params(("parallel", "arbitrary")),
        name="moe_experts",
    )(block_expert, n_valid, xs, w_gate, w_up, w_down)


def _combine_kernel(dest_ref, ys_ref, w_ref, x_ref, mt_ref, g_ref, o_ref, buf, sem):
    tc = x_ref.shape[0]

    def row_copy(t, k):
        return pltpu.make_async_copy(ys_ref.at[pl.ds(dest_ref[0, k * tc + t], 1)], buf.at[k, pl.ds(t, 1)], sem)

    def issue(t, carry):
        row_copy(t, 0).start()
        row_copy(t, 1).start()
        return carry

    def drain(t, carry):
        row_copy(t, 0).wait()
        row_copy(t, 1).wait()
        return carry

    lax.fori_loop(0, tc, issue, 0)
    lax.fori_loop(0, tc, drain, 0)
    w = w_ref[...]
    f = w[:, 0:1] * buf[0] + w[:, 1:2] * buf[1]
    for s in range(tc // MOD_BLOCK):
        rows = slice(s * MOD_BLOCK, (s + 1) * MOD_BLOCK)
        o_ref[rows, :] = x_ref[rows, :] + mt_ref[s][5:6] * _rms(f[rows, :], g_ref[...])


def _combine_call(dest_blocks, ys, w_cols, x, modtab, g_post):
    n, d = x.shape
    tc = COMBINE_ROWS
    return pl.pallas_call(
        _combine_kernel,
        grid=(n // tc,),
        in_specs=[
            pl.BlockSpec((None, 1, 2 * tc), lambda i: (i, 0, 0), memory_space=pltpu.SMEM),
            pl.BlockSpec(memory_space=pl.ANY),
            pl.BlockSpec((tc, 2), lambda i: (i, 0)),
            pl.BlockSpec((tc, d), lambda i: (i, 0)),
            pl.BlockSpec((tc // MOD_BLOCK, 8, d), lambda i: (i, 0, 0)),
            pl.BlockSpec((1, d), lambda i: (0, 0)),
        ],
        out_specs=pl.BlockSpec((tc, d), lambda i: (i, 0)),
        out_shape=jax.ShapeDtypeStruct((n, d), _F32),
        scratch_shapes=[pltpu.VMEM((2, tc, d), _F32), pltpu.SemaphoreType.DMA(())],
        compiler_params=_params(("arbitrary",)),
        name="moe_combine",
    )(dest_blocks, ys, w_cols, x, modtab, g_post)


def _dest_blocks(dest, rows):
    n = dest.shape[1]
    return dest.reshape(2, n // rows, rows).transpose(1, 0, 2).reshape(n // rows, 1, 2 * rows)


def _moe_call(f, route, x, modtab, g_post, w_gate, w_up, w_down):
    n, d = x.shape
    tb = EXPERT_ROWS
    rank, counts = _rank_call(route)
    expert = route[0:2].astype(jnp.int32)
    counts = counts[:, 0].astype(jnp.int32)
    padded = (counts + tb - 1) // tb * tb
    pad_end = jnp.cumsum(padded)
    pad_start = pad_end - padded
    dest = rank[0:2].astype(jnp.int32) + sum(jnp.where(expert == e, pad_start[e], 0) for e in range(N_EXPERTS))
    n_blocks = -(-(2 * n + N_EXPERTS * (tb - 1)) // tb)
    block_start = jnp.arange(n_blocks, dtype=jnp.int32) * tb
    block_expert = jnp.minimum(jnp.searchsorted(pad_end, block_start, side="right"), N_EXPERTS - 1).astype(jnp.int32)
    n_valid = (pad_end[-1:] // tb).astype(jnp.int32)
    xs = _dispatch_call(_dest_blocks(dest, TOKEN_TILE), f, jnp.zeros((n_blocks * tb, d), _F32))
    ys = _expert_call(block_expert, n_valid, xs, w_gate, w_up, w_down)
    return _combine_call(_dest_blocks(dest, COMBINE_ROWS), ys, route[2:4].T, x, modtab, g_post)


def kernel(x, c, ctx, c_ctx, w_mod, b_mod, g_pre_mix, g_post_mix, g_pre_ffn, g_post_ffn, w_in, w_out, g_vnorm, w_spatial, b_spatial, w_decay, b_decay, g_gla_norm, w_ffn_gate, w_ffn_up, w_ffn_down, w_router, w_exp_gate, w_exp_up, w_exp_down):
    n_batch, seq, d = x.shape
    ctx_len = ctx.shape[1]
    depth = w_mod.shape[0]
    a_w = A_HEADS * A_HEAD_DIM
    k_w = B_HEADS * B_DK
    v_w = B_HEADS * B_DV
    main_w = 2 * a_w + 2 * k_w + 2 * v_w
    assert w_in.shape[2] == main_w + 2 * DECAY_RANK and seq % TOKEN_TILE == 0
    assert (n_batch * ctx_len) % TOKEN_TILE == 0 and ctx_len % MOD_BLOCK == 0

    stream = jnp.concatenate([ctx.reshape(n_batch * ctx_len, d), x.reshape(n_batch * seq, d)], axis=0)

    cvec = jnp.zeros((ROUTE_ROWS, d), _F32).at[:n_batch].set(c).at[n_batch].set(c_ctx)
    mod = _mod_call(cvec, w_mod, b_mod).reshape(depth, ROUTE_ROWS, 6, d)
    per_block = lambda r, blocks: jnp.broadcast_to(mod[:, r:r + 1], (depth, blocks, 6, d))
    modtab = jnp.concatenate([per_block(n_batch, n_batch * ctx_len // MOD_BLOCK)]
                             + [per_block(b, seq // MOD_BLOCK) for b in range(n_batch)], axis=1)
    modtab = jnp.pad(modtab, ((0, 0), (0, 0), (0, 2), (0, 0)))

    w_main = w_in[:, :, :main_w].astype(_BF16)
    w_a = jnp.zeros((depth, d, 2 * CHUNK), _F32)
    w_a = w_a.at[:, :, :DECAY_RANK].set(w_in[:, :, main_w:main_w + DECAY_RANK])
    w_a = w_a.at[:, :, CHUNK:CHUNK + DECAY_RANK].set(w_in[:, :, main_w + DECAY_RANK:]).astype(_BF16)
    wd = jnp.pad(w_decay, ((0, 0), (0, 0), (0, CHUNK - DECAY_RANK), (0, 0)))
    wdt = wd.transpose(0, 1, 3, 2)
    w_out_b = w_out.astype(_BF16)
    w_s = w_spatial.astype(_BF16)

    row = lambda g: g.reshape(1, -1)
    for l in range(depth):
        uv, q, gate, k, v, a = _inproj_call(stream, modtab[l], row(g_pre_mix[l]), w_main[l], w_a[l])
        ya = _gmlp_call(uv, row(g_vnorm[l]), w_s[l], b_spatial[l][:, :, None])
        o_f, o_b = _gla_call(q, k, v, a, wd[l], b_decay[l][:, None, :], wdt[l], b_decay[l][:, :, None],
                             n_batch, ctx_len // CHUNK, seq // CHUNK)
        routed = l % 2 == 1
        outs = _outproj_call(ya, o_f, o_b, gate, stream, modtab[l], row(g_gla_norm[l]), row(g_post_mix[l]),
                             row(g_pre_ffn[l]), w_out_b[l], w_router[l // 2].T if routed else None)
        if routed:
            stream, f, route = outs
            stream = _moe_call(f, route, stream, modtab[l], row(g_post_ffn[l]),
                               w_exp_gate[l // 2].astype(_BF16), w_exp_up[l // 2].astype(_BF16),
                               w_exp_down[l // 2].astype(_BF16))
        else:
            stream, f = outs
            stream = _ffn_call(f, w_ffn_gate[l // 2].astype(_BF16), w_ffn_up[l // 2].astype(_BF16),
                               w_ffn_down[l // 2].astype(_BF16), stream, modtab[l], row(g_post_ffn[l]))
    return stream[n_batch * ctx_len:].reshape(n_batch, seq, d)
```

```python
import functools

import jax
import jax.numpy as jnp
from jax import lax
from jax.experimental import pallas as pl
from jax.experimental.pallas import tpu as pltpu

_F32 = jnp.float32
_BF16 = jnp.bfloat16

EPS = 1e-6
CHUNK = 128
MOD_BLOCK = 256
A_HEADS = 8
A_HEAD_DIM = 128
B_HEADS = 4
B_DK = 128
B_DV = 256
DECAY_RANK = 16
DECAY_TAU = 16.0
N_EXPERTS = 8
ROUTE_ROWS = 8
TOKEN_TILE = 512
FFN_TOKEN_TILE = 768
FF_TILE = 512
EXPERT_ROWS = 512
COMBINE_ROWS = 256
VMEM_LIMIT = 56 << 20


def _dot(a, b):
    return jnp.dot(a, b, preferred_element_type=_F32)


def _dot_nt(a, b):
    return lax.dot_general(a, b, (((1,), (1,)), ((), ())), preferred_element_type=_F32)


def _dot_tn(a, b):
    return lax.dot_general(a, b, (((0,), (0,)), ((), ())), preferred_element_type=_F32)


def _split(x):
    hi = x.astype(_BF16)
    lo = (x - hi.astype(_F32)).astype(_BF16)
    return hi, lo


def _dot_f32(a, b, dot=_dot):
    ah, al = _split(a)
    bh, bl = _split(b)
    return dot(ah, bh) + dot(ah, bl) + dot(al, bh)


def _rms(x, g):
    return x * lax.rsqrt(jnp.mean(x * x, axis=-1, keepdims=True) + EPS) * g


def _gelu(x):
    return 0.5 * x * (1.0 + lax.erf(x * (2.0 ** -0.5)))


def _silu(x):
    return x * jax.nn.sigmoid(x)


def _log_sigmoid(z):
    return jnp.minimum(z, 0.0) - jnp.log1p(jnp.exp(-jnp.abs(z)))


def _params(semantics):
    return pltpu.CompilerParams(dimension_semantics=semantics, vmem_limit_bytes=VMEM_LIMIT)


def _resident(shape):
    zeros = (0,) * len(shape)
    return pl.BlockSpec(shape, lambda *_: zeros, pipeline_mode=pl.Buffered(1))


def _mod_kernel(c_ref, w_ref, b_ref, o_ref):
    s = _silu(c_ref[...]).astype(_BF16)
    o_ref[0] = _dot(s, w_ref[0].astype(_BF16)) + b_ref[0]


def _mod_call(cvec, w_mod, b_mod):
    depth, d, n = w_mod.shape
    tn = 1024
    return pl.pallas_call(
        _mod_kernel,
        grid=(depth, n // tn),
        in_specs=[
            pl.BlockSpec((ROUTE_ROWS, d), lambda l, j: (0, 0)),
            pl.BlockSpec((1, d, tn), lambda l, j: (l, 0, j)),
            pl.BlockSpec((1, 1, tn), lambda l, j: (l, 0, j)),
        ],
        out_specs=pl.BlockSpec((1, ROUTE_ROWS, tn), lambda l, j: (l, 0, j)),
        out_shape=jax.ShapeDtypeStruct((depth, ROUTE_ROWS, n), _F32),
        compiler_params=_params(("parallel", "parallel")),
        name="adaln_mod",
    )(cvec, w_mod, b_mod.reshape(depth, 1, n))


def _scan_masks(direction):
    row = lax.broadcasted_iota(jnp.int32, (CHUNK, CHUNK), 0)
    col = lax.broadcasted_iota(jnp.int32, (CHUNK, CHUNK), 1)
    return ((col <= row), CHUNK - 1) if direction == 0 else ((col >= row), 0)


def _inproj_kernel(x_ref, mt_ref, g_ref, w_ref, wa_ref, wd_ref, bd_ref,
                   uv_ref, gate_ref, v_ref, qf_ref, kf_ref, qb_ref, kb_ref, cols_ref, h_scr):
    tm = x_ref.shape[0]
    a_w = A_HEADS * A_HEAD_DIM
    k_w = B_HEADS * B_DK
    v_w = B_HEADS * B_DV
    for s in range(tm // MOD_BLOCK):
        rows = slice(s * MOD_BLOCK, (s + 1) * MOD_BLOCK)
        mt = mt_ref[s]
        h = _rms(x_ref[rows, :], g_ref[...]) * (1.0 + mt[1:2]) + mt[0:1]
        h_scr[rows, :] = h.astype(_BF16)
    h = h_scr[...]
    off_q = 2 * a_w
    off_gate = off_q + k_w
    off_k = off_gate + v_w
    off_v = off_k + k_w
    q = _dot(h, w_ref[:, off_q:off_q + k_w]) * (B_DK ** -0.5)
    k = _dot(h, w_ref[:, off_k:off_k + k_w])
    a = _dot(h, wa_ref[...])
    sub = lax.broadcasted_iota(jnp.int32, (CHUNK, k_w), 0)
    factor_rows = [jnp.zeros((CHUNK, k_w), _F32) for _ in range(tm // CHUNK)]
    for d, (q_ref, k_ref) in enumerate(((qf_ref, kf_ref), (qb_ref, kb_ref))):
        scanned, end = _scan_masks(d)
        tri = jnp.where(scanned, 1.0, 0.0).astype(_BF16)
        la = _log_sigmoid(_dot_f32(a[:, d * CHUNK:(d + 1) * CHUNK], wd_ref[d]) + bd_ref[d]) * (1.0 / DECAY_TAU)
        la_hi, la_lo = _split(la)
        for c in range(tm // CHUNK):
            rows = slice(c * CHUNK, (c + 1) * CHUNK)
            b = _dot(tri, la_hi[rows]) + _dot(tri, la_lo[rows])
            b_mid = b[CHUNK // 2:CHUNK // 2 + 1]
            b_end = b[end:end + 1]
            q_ref[rows, :] = (q[rows] * jnp.exp(b - b_mid)).astype(_BF16)
            k_ref[rows, :] = (k[rows] * jnp.exp(b_mid - b)).astype(_BF16)
            for r, factor in enumerate((jnp.exp(b_mid), jnp.exp(b_end - b_mid), jnp.exp(b_end))):
                factor_rows[c] = jnp.where(sub == 3 * d + r, factor, factor_rows[c])
    for c in range(tm // CHUNK):
        for hd in range(B_HEADS):
            kc = slice(hd * B_DK, (hd + 1) * B_DK)
            cols_ref[c, kc, :] = factor_rows[c][:, kc].T
    for c0 in range(0, 2 * a_w, a_w):
        uv_ref[:, c0:c0 + a_w] = _dot(h, w_ref[:, c0:c0 + a_w]).astype(_BF16)
    gate_ref[...] = _dot(h, w_ref[:, off_gate:off_gate + v_w]).astype(_BF16)
    v_ref[...] = _dot(h, w_ref[:, off_v:off_v + v_w]).astype(_BF16)


def _inproj_call(x, modtab, g_pre, w_main, w_a, wd, bd):
    n, d = x.shape
    tm = TOKEN_TILE
    a_w = A_HEADS * A_HEAD_DIM
    k_w = B_HEADS * B_DK
    v_w = B_HEADS * B_DV
    widths = (2 * a_w, v_w, v_w, k_w, k_w, k_w, k_w)
    tok = lambda w: pl.BlockSpec((tm, w), lambda i: (i, 0))
    return pl.pallas_call(
        _inproj_kernel,
        grid=(n // tm,),
        in_specs=[
            tok(d),
            pl.BlockSpec((tm // MOD_BLOCK, 8, d), lambda i: (i, 0, 0)),
            pl.BlockSpec((1, d), lambda i: (0, 0)),
            _resident(w_main.shape),
            _resident(w_a.shape),
            pl.BlockSpec(wd.shape, lambda i: (0, 0, 0)),
            pl.BlockSpec(bd.shape, lambda i: (0, 0, 0)),
        ],
        out_specs=[tok(w) for w in widths] + [pl.BlockSpec((tm // CHUNK, k_w, CHUNK), lambda i: (i, 0, 0))],
        out_shape=[jax.ShapeDtypeStruct((n, w), _BF16) for w in widths]
        + [jax.ShapeDtypeStruct((n // CHUNK, k_w, CHUNK), _F32)],
        scratch_shapes=[pltpu.VMEM((tm, d), _BF16)],
        compiler_params=_params(("parallel",)),
        name="in_proj",
    )(x, modtab, g_pre, w_main, w_a, wd, bd)


def _gmlp_kernel(uv_ref, gv_ref, ws_ref, bs_ref, y_ref):
    tm = uv_ref.shape[0]
    a_w = A_HEADS * A_HEAD_DIM
    for c in range(tm // CHUNK):
        rows = slice(c * CHUNK, (c + 1) * CHUNK)
        u = _gelu(uv_ref[rows, :a_w].astype(_F32))
        v = _rms(_gelu(uv_ref[rows, a_w:].astype(_F32)), gv_ref[...]).astype(_BF16)
        for h in range(A_HEADS):
            cols = slice(h * A_HEAD_DIM, (h + 1) * A_HEAD_DIM)
            s = _dot(ws_ref[h], v[:, cols]) + bs_ref[h]
            y_ref[rows, cols] = (u[:, cols] * s).astype(_BF16)


def _gmlp_call(uv, g_v, w_s, b_s):
    n = uv.shape[0]
    tm = TOKEN_TILE
    a_w = A_HEADS * A_HEAD_DIM
    return pl.pallas_call(
        _gmlp_kernel,
        grid=(n // tm,),
        in_specs=[
            pl.BlockSpec((tm, 2 * a_w), lambda i: (i, 0)),
            pl.BlockSpec((1, a_w), lambda i: (0, 0)),
            pl.BlockSpec((A_HEADS, CHUNK, CHUNK), lambda i: (0, 0, 0)),
            pl.BlockSpec((A_HEADS, CHUNK, 1), lambda i: (0, 0, 0)),
        ],
        out_specs=pl.BlockSpec((tm, a_w), lambda i: (i, 0)),
        out_shape=jax.ShapeDtypeStruct((n, a_w), _BF16),
        compiler_params=_params(("parallel",)),
        name="gmlp",
    )(uv, g_v, w_s, b_s)


def _gla_kernel(qf_ref, kf_ref, vf_ref, cf_ref, qb_ref, kb_ref, vb_ref, cb_ref, of_ref, ob_ref, s_ref):
    @pl.when(pl.program_id(1) == 0)
    def _():
        s_ref[...] = jnp.zeros_like(s_ref)

    streams = ((qf_ref, kf_ref, vf_ref, cf_ref, of_ref), (qb_ref, kb_ref, vb_ref, cb_ref, ob_ref))
    for d, (q_ref, k_ref, v_ref, c_ref, o_ref) in enumerate(streams):
        scanned, _ = _scan_masks(d)
        for h in range(B_HEADS):
            kc = slice(h * B_DK, (h + 1) * B_DK)
            vc = slice(h * B_DV, (h + 1) * B_DV)
            cols = c_ref[0, kc, :]
            e_mid, e_end, decay = (cols[:, 3 * d + r:3 * d + r + 1] for r in range(3))
            qd = q_ref[:, kc]
            kd = k_ref[:, kc]
            vh = v_ref[:, vc]
            state = s_ref[d, h]
            scores = jnp.where(scanned, _dot_nt(qd, kd), 0.0).astype(_BF16)
            o_ref[:, vc] = _dot(scores, vh) + _dot(qd, (e_mid * state).astype(_BF16))
            s_ref[d, h] = decay * state + e_end * _dot_tn(kd, vh)


def _gla_call(qf, kf, qb, kb, v, cols, n_batch, ctx_chunks, lat_chunks):
    n = v.shape[0]
    k_w = B_HEADS * B_DK
    v_w = B_HEADS * B_DV
    steps = ctx_chunks + lat_chunks
    lat0 = n_batch * ctx_chunks

    def fwd(b, t):
        return jnp.where(t < ctx_chunks, b * ctx_chunks + t, lat0 + b * lat_chunks + (t - ctx_chunks))

    def bwd(b, t):
        return jnp.where(t < ctx_chunks, b * ctx_chunks + (ctx_chunks - 1 - t),
                         lat0 + b * lat_chunks + (steps - 1 - t))

    def specs(chunk_of):
        tok = lambda w: pl.BlockSpec((CHUNK, w), lambda b, t: (chunk_of(b, t), 0))
        return [tok(k_w), tok(k_w), tok(v_w), pl.BlockSpec((1, k_w, CHUNK), lambda b, t: (chunk_of(b, t), 0, 0))]

    return pl.pallas_call(
        _gla_kernel,
        grid=(n_batch, steps),
        in_specs=specs(fwd) + specs(bwd),
        out_specs=[pl.BlockSpec((CHUNK, v_w), lambda b, t: (fwd(b, t), 0)),
                   pl.BlockSpec((CHUNK, v_w), lambda b, t: (bwd(b, t), 0))],
        out_shape=[jax.ShapeDtypeStruct((n, v_w), _F32)] * 2,
        scratch_shapes=[pltpu.VMEM((2, B_HEADS, B_DK, B_DV), _F32)],
        compiler_params=_params(("parallel", "arbitrary")),
        name="gla_scan",
    )(qf, kf, v, cols, qb, kb, v, cols)


def _outproj_kernel(*refs, routed):
    if routed:
        (ya_ref, of_ref, ob_ref, gate_ref, x_ref, mt_ref, gn_ref, gpost_ref, gffn_ref, w_ref, wr_ref,
         xo_ref, f_ref, route_ref, y_scr) = refs
    else:
        (ya_ref, of_ref, ob_ref, gate_ref, x_ref, mt_ref, gn_ref, gpost_ref, gffn_ref, w_ref,
         xo_ref, f_ref, y_scr) = refs
    a_w = A_HEADS * A_HEAD_DIM
    tm = x_ref.shape[0]
    y_scr[:, :a_w] = ya_ref[...]
    for h in range(B_HEADS):
        vc = slice(h * B_DV, (h + 1) * B_DV)
        o = of_ref[:, vc] + ob_ref[:, vc]
        o = o * lax.rsqrt(jnp.mean(o * o, axis=-1, keepdims=True) + EPS) * gn_ref[:, vc]
        y_scr[:, a_w + h * B_DV:a_w + (h + 1) * B_DV] = (o * _silu(gate_ref[:, vc].astype(_F32))).astype(_BF16)
    m = _rms(_dot(y_scr[...], w_ref[...]), gpost_ref[...])
    for s in range(tm // MOD_BLOCK):
        rows = slice(s * MOD_BLOCK, (s + 1) * MOD_BLOCK)
        mt = mt_ref[s]
        x = x_ref[rows, :] + mt[2:3] * m[rows, :]
        xo_ref[rows, :] = x
        f = _rms(x, gffn_ref[...]) * (1.0 + mt[4:5]) + mt[3:4]
        f_ref[rows, :] = f.astype(f_ref.dtype)
    if routed:
        logits = _dot_f32(wr_ref[...], f_ref[...], _dot_nt)
        eidx = lax.broadcasted_iota(jnp.int32, logits.shape, 0).astype(_F32)
        none = float(N_EXPERTS)
        m1 = jnp.max(logits, axis=0, keepdims=True)
        i1 = jnp.min(jnp.where(logits == m1, eidx, none), axis=0, keepdims=True)
        rest = jnp.where(eidx == i1, -jnp.inf, logits)
        m2 = jnp.max(rest, axis=0, keepdims=True)
        i2 = jnp.min(jnp.where(rest == m2, eidx, none), axis=0, keepdims=True)
        e2 = jnp.exp(m2 - m1)
        w1 = 1.0 / (1.0 + e2)
        w2 = e2 / (1.0 + e2)
        r = lax.broadcasted_iota(jnp.int32, logits.shape, 0)
        route_ref[...] = jnp.where(r == 0, i1, jnp.where(r == 1, i2, jnp.where(r == 2, w1, jnp.where(r == 3, w2, 0.0))))


def _outproj_call(ya, o_f, o_b, gate, x, modtab, g_norm, g_post, g_ffn, w_out, wr_t):
    n, d = x.shape
    tm = TOKEN_TILE
    routed = wr_t is not None
    tok = lambda w: pl.BlockSpec((tm, w), lambda i: (i, 0))
    vec = lambda w: pl.BlockSpec((1, w), lambda i: (0, 0))
    a_w = A_HEADS * A_HEAD_DIM
    v_w = B_HEADS * B_DV
    in_specs = [tok(a_w), tok(v_w), tok(v_w), tok(v_w), tok(d),
                pl.BlockSpec((tm // MOD_BLOCK, 8, d), lambda i: (i, 0, 0)),
                vec(v_w), vec(d), vec(d), _resident(w_out.shape)]
    args = [ya, o_f, o_b, gate, x, modtab, g_norm, g_post, g_ffn, w_out]
    out_specs = [tok(d), tok(d)]
    out_shape = [jax.ShapeDtypeStruct((n, d), _F32), jax.ShapeDtypeStruct((n, d), _F32 if routed else _BF16)]
    if routed:
        in_specs.append(pl.BlockSpec(wr_t.shape, lambda i: (0, 0)))
        args.append(wr_t)
        out_specs.append(pl.BlockSpec((ROUTE_ROWS, tm), lambda i: (0, i)))
        out_shape.append(jax.ShapeDtypeStruct((ROUTE_ROWS, n), _F32))
    return pl.pallas_call(
        functools.partial(_outproj_kernel, routed=routed),
        grid=(n // tm,),
        in_specs=in_specs,
        out_specs=out_specs,
        out_shape=out_shape,
        scratch_shapes=[pltpu.VMEM((tm, a_w + v_w), _BF16)],
        compiler_params=_params(("parallel",)),
        name="out_proj_routed" if routed else "out_proj",
    )(*args)


def _ffn_kernel(f_ref, wg_ref, wu_ref, wd_ref, x_ref, mt_ref, g_ref, o_ref):
    j = pl.program_id(1)

    @pl.when(j == 0)
    def _():
        o_ref[...] = jnp.zeros_like(o_ref)

    f = f_ref[...]
    act = (_silu(_dot(f, wg_ref[...])) * _dot(f, wu_ref[...])).astype(_BF16)
    o_ref[...] += _dot(act, wd_ref[...])

    @pl.when(j == pl.num_programs(1) - 1)
    def _():
        for s in range(o_ref.shape[0] // MOD_BLOCK):
            rows = slice(s * MOD_BLOCK, (s + 1) * MOD_BLOCK)
            o_ref[rows, :] = x_ref[rows, :] + mt_ref[s][5:6] * _rms(o_ref[rows, :], g_ref[...])


def _token_tile(n, target):
    tm = target - target % MOD_BLOCK
    while n % tm:
        tm -= MOD_BLOCK
    return tm


def _ffn_call(f, w_gate, w_up, w_down, x, modtab, g_post):
    n, d = x.shape
    d_ff = w_gate.shape[1]
    tm = _token_tile(n, FFN_TOKEN_TILE)
    tf = FF_TILE
    return pl.pallas_call(
        _ffn_kernel,
        grid=(n // tm, d_ff // tf),
        in_specs=[
            pl.BlockSpec((tm, d), lambda i, j: (i, 0)),
            pl.BlockSpec((d, tf), lambda i, j: (0, j)),
            pl.BlockSpec((d, tf), lambda i, j: (0, j)),
            pl.BlockSpec((tf, d), lambda i, j: (j, 0)),
            pl.BlockSpec((tm, d), lambda i, j: (i, 0)),
            pl.BlockSpec((tm // MOD_BLOCK, 8, d), lambda i, j: (i, 0, 0)),
            pl.BlockSpec((1, d), lambda i, j: (0, 0)),
        ],
        out_specs=pl.BlockSpec((tm, d), lambda i, j: (i, 0)),
        out_shape=jax.ShapeDtypeStruct((n, d), _F32),
        compiler_params=_params(("parallel", "arbitrary")),
        name="dense_ffn",
    )(f, w_gate, w_up, w_down, x, modtab, g_post)


def _rank_kernel(route_ref, rank_ref, count_ref, carry):
    @pl.when(pl.program_id(0) == 0)
    def _():
        carry[...] = jnp.zeros_like(carry)

    tl = route_ref.shape[1]
    r = route_ref[...]
    eidx = lax.broadcasted_iota(jnp.int32, r.shape, 0).astype(_F32)
    oh1 = jnp.where(eidx == r[0:1], 1.0, 0.0)
    oh2 = jnp.where(eidx == r[1:2], 1.0, 0.0)
    both = oh1 + oh2
    before = lax.broadcasted_iota(jnp.int32, (tl, tl), 0) < lax.broadcasted_iota(jnp.int32, (tl, tl), 1)
    seen = carry[:, 0:1] + _dot(both.astype(_BF16), jnp.where(before, 1.0, 0.0).astype(_BF16))
    rank1 = jnp.sum(oh1 * seen, axis=0, keepdims=True)
    rank2 = jnp.sum(oh2 * seen, axis=0, keepdims=True)
    row = lax.broadcasted_iota(jnp.int32, r.shape, 0)
    rank_ref[...] = jnp.where(row == 0, rank1, jnp.where(row == 1, rank2, 0.0))
    carry[...] = carry[...] + jnp.sum(both, axis=1, keepdims=True)
    count_ref[...] = carry[...]


def _rank_call(route):
    n = route.shape[1]
    tl = TOKEN_TILE
    return pl.pallas_call(
        _rank_kernel,
        grid=(n // tl,),
        in_specs=[pl.BlockSpec((ROUTE_ROWS, tl), lambda i: (0, i))],
        out_specs=[pl.BlockSpec((ROUTE_ROWS, tl), lambda i: (0, i)),
                   pl.BlockSpec((ROUTE_ROWS, 128), lambda i: (0, 0))],
        out_shape=[jax.ShapeDtypeStruct((ROUTE_ROWS, n), _F32),
                   jax.ShapeDtypeStruct((ROUTE_ROWS, 128), _F32)],
        scratch_shapes=[pltpu.VMEM((ROUTE_ROWS, 128), _F32)],
        compiler_params=_params(("arbitrary",)),
        name="moe_rank",
    )(route)


def _dispatch_kernel(dest_ref, f_ref, xs_in_ref, xs_ref, sem):
    del xs_in_ref
    td = f_ref.shape[0]

    def row_copy(t, k):
        return pltpu.make_async_copy(f_ref.at[pl.ds(t, 1)], xs_ref.at[pl.ds(dest_ref[0, k * td + t], 1)], sem)

    def issue(t, carry):
        row_copy(t, 0).start()
        row_copy(t, 1).start()
        return carry

    def drain(t, carry):
        row_copy(t, 0).wait()
        row_copy(t, 1).wait()
        return carry

    lax.fori_loop(0, td, issue, 0)
    lax.fori_loop(0, td, drain, 0)


def _dispatch_call(dest_blocks, f, xs_init):
    n_steps = dest_blocks.shape[0]
    return pl.pallas_call(
        _dispatch_kernel,
        grid=(n_steps,),
        in_specs=[
            pl.BlockSpec((None, 1, dest_blocks.shape[2]), lambda i: (i, 0, 0), memory_space=pltpu.SMEM),
            pl.BlockSpec((dest_blocks.shape[2] // 2, f.shape[1]), lambda i: (i, 0)),
            pl.BlockSpec(memory_space=pl.ANY),
        ],
        out_specs=pl.BlockSpec(memory_space=pl.ANY),
        out_shape=jax.ShapeDtypeStruct(xs_init.shape, xs_init.dtype),
        scratch_shapes=[pltpu.SemaphoreType.DMA(())],
        input_output_aliases={2: 0},
        compiler_params=_params(("arbitrary",)),
        name="moe_dispatch",
    )(dest_blocks, f, xs_init)


def _expert_kernel(be_ref, nv_ref, xs_ref, wg_ref, wu_ref, wd_ref, ys_ref, xb_scr):
    del be_ref
    i = pl.program_id(0)
    j = pl.program_id(1)

    @pl.when(j == 0)
    def _():
        ys_ref[...] = jnp.zeros_like(ys_ref)
        xb_scr[...] = xs_ref[...].astype(_BF16)

    @pl.when(i < nv_ref[0])
    def _():
        xb = xb_scr[...]
        act = (_silu(_dot(xb, wg_ref[...])) * _dot(xb, wu_ref[...])).astype(_BF16)
        ys_ref[...] += _dot(act, wd_ref[...])


def _expert_call(block_expert, n_valid, xs, w_gate, w_up, w_down):
    cap, d = xs.shape
    d_ff = w_gate.shape[2]
    tb = EXPERT_ROWS
    tf = FF_TILE
    n_ff = d_ff // tf

    def ff_of(i, j, nv):
        return jnp.where(i < nv[0], j, n_ff - 1)

    grid_spec = pltpu.PrefetchScalarGridSpec(
        num_scalar_prefetch=2,
        grid=(cap // tb, n_ff),
        in_specs=[
            pl.BlockSpec((tb, d), lambda i, j, be, nv: (i, 0)),
            pl.BlockSpec((None, d, tf), lambda i, j, be, nv: (be[i], 0, ff_of(i, j, nv))),
            pl.BlockSpec((None, d, tf), lambda i, j, be, nv: (be[i], 0, ff_of(i, j, nv))),
            pl.BlockSpec((None, tf, d), lambda i, j, be, nv: (be[i], ff_of(i, j, nv), 0)),
        ],
        out_specs=pl.BlockSpec((tb, d), lambda i, j, be, nv: (i, 0)),
        scratch_shapes=[pltpu.VMEM((tb, d), _BF16)],
    )
    return pl.pallas_call(
        _expert_kernel,
        grid_spec=grid_spec,
        out_shape=jax.ShapeDtypeStruct((cap, d), _F32),
        compiler_params=_params(("parallel", "arbitrary")),
        name="moe_experts",
    )(block_expert, n_valid, xs, w_gate, w_up, w_down)


def _combine_kernel(dest_ref, ys_ref, w_ref, x_ref, mt_ref, g_ref, o_ref, buf, sem):
    tc = x_ref.shape[0]

    def row_copy(t, k):
        return pltpu.make_async_copy(ys_ref.at[pl.ds(dest_ref[0, k * tc + t], 1)], buf.at[k, pl.ds(t, 1)], sem)

    def issue(t, carry):
        row_copy(t, 0).start()
        row_copy(t, 1).start()
        return carry

    def drain(t, carry):
        row_copy(t, 0).wait()
        row_copy(t, 1).wait()
        return carry

    lax.fori_loop(0, tc, issue, 0)
    lax.fori_loop(0, tc, drain, 0)
    w = w_ref[...]
    f = w[:, 0:1] * buf[0] + w[:, 1:2] * buf[1]
    for s in range(tc // MOD_BLOCK):
        rows = slice(s * MOD_BLOCK, (s + 1) * MOD_BLOCK)
        o_ref[rows, :] = x_ref[rows, :] + mt_ref[s][5:6] * _rms(f[rows, :], g_ref[...])


def _combine_call(dest_blocks, ys, w_cols, x, modtab, g_post):
    n, d = x.shape
    tc = COMBINE_ROWS
    return pl.pallas_call(
        _combine_kernel,
        grid=(n // tc,),
        in_specs=[
            pl.BlockSpec((None, 1, 2 * tc), lambda i: (i, 0, 0), memory_space=pltpu.SMEM),
            pl.BlockSpec(memory_space=pl.ANY),
            pl.BlockSpec((tc, 2), lambda i: (i, 0)),
            pl.BlockSpec((tc, d), lambda i: (i, 0)),
            pl.BlockSpec((tc // MOD_BLOCK, 8, d), lambda i: (i, 0, 0)),
            pl.BlockSpec((1, d), lambda i: (0, 0)),
        ],
        out_specs=pl.BlockSpec((tc, d), lambda i: (i, 0)),
        out_shape=jax.ShapeDtypeStruct((n, d), _F32),
        scratch_shapes=[pltpu.VMEM((2, tc, d), _F32), pltpu.SemaphoreType.DMA(())],
        compiler_params=_params(("arbitrary",)),
        name="moe_combine",
    )(dest_blocks, ys, w_cols, x, modtab, g_post)


def _dest_blocks(dest, rows):
    n = dest.shape[1]
    return dest.reshape(2, n // rows, rows).transpose(1, 0, 2).reshape(n // rows, 1, 2 * rows)


def _moe_call(f, route, x, modtab, g_post, w_gate, w_up, w_down):
    n, d = x.shape
    tb = EXPERT_ROWS
    rank, counts = _rank_call(route)
    expert = route[0:2].astype(jnp.int32)
    counts = counts[:, 0].astype(jnp.int32)
    padded = (counts + tb - 1) // tb * tb
    pad_end = jnp.cumsum(padded)
    pad_start = pad_end - padded
    dest = rank[0:2].astype(jnp.int32) + sum(jnp.where(expert == e, pad_start[e], 0) for e in range(N_EXPERTS))
    n_blocks = -(-(2 * n + N_EXPERTS * (tb - 1)) // tb)
    block_start = jnp.arange(n_blocks, dtype=jnp.int32) * tb
    block_expert = jnp.minimum(jnp.sum(block_start[:, None] >= pad_end[None, :], axis=1), N_EXPERTS - 1).astype(jnp.int32)
    n_valid = (pad_end[-1:] // tb).astype(jnp.int32)
    xs = _dispatch_call(_dest_blocks(dest, TOKEN_TILE), f, jnp.zeros((n_blocks * tb, d), _F32))
    ys = _expert_call(block_expert, n_valid, xs, w_gate, w_up, w_down)
    return _combine_call(_dest_blocks(dest, COMBINE_ROWS), ys, route[2:4].T, x, modtab, g_post)


def kernel(x, c, ctx, c_ctx, w_mod, b_mod, g_pre_mix, g_post_mix, g_pre_ffn, g_post_ffn, w_in, w_out, g_vnorm, w_spatial, b_spatial, w_decay, b_decay, g_gla_norm, w_ffn_gate, w_ffn_up, w_ffn_down, w_router, w_exp_gate, w_exp_up, w_exp_down):
    n_batch, seq, d = x.shape
    ctx_len = ctx.shape[1]
    depth = w_mod.shape[0]
    a_w = A_HEADS * A_HEAD_DIM
    k_w = B_HEADS * B_DK
    v_w = B_HEADS * B_DV
    main_w = 2 * a_w + 2 * k_w + 2 * v_w
    assert w_in.shape[2] == main_w + 2 * DECAY_RANK and seq % TOKEN_TILE == 0
    assert (n_batch * ctx_len) % TOKEN_TILE == 0 and ctx_len % MOD_BLOCK == 0

    stream = jnp.concatenate([ctx.reshape(n_batch * ctx_len, d), x.reshape(n_batch * seq, d)], axis=0)

    cvec = jnp.zeros((ROUTE_ROWS, d), _F32).at[:n_batch].set(c).at[n_batch].set(c_ctx)
    mod = _mod_call(cvec, w_mod, b_mod).reshape(depth, ROUTE_ROWS, 6, d)
    per_block = lambda r, blocks: jnp.broadcast_to(mod[:, r:r + 1], (depth, blocks, 6, d))
    modtab = jnp.concatenate([per_block(n_batch, n_batch * ctx_len // MOD_BLOCK)]
                             + [per_block(b, seq // MOD_BLOCK) for b in range(n_batch)], axis=1)
    modtab = jnp.pad(modtab, ((0, 0), (0, 0), (0, 2), (0, 0)))

    w_main = w_in[:, :, :main_w].astype(_BF16)
    w_a = jnp.zeros((depth, d, 2 * CHUNK), _F32)
    w_a = w_a.at[:, :, :DECAY_RANK].set(w_in[:, :, main_w:main_w + DECAY_RANK])
    w_a = w_a.at[:, :, CHUNK:CHUNK + DECAY_RANK].set(w_in[:, :, main_w + DECAY_RANK:]).astype(_BF16)
    wd = jnp.pad(w_decay, ((0, 0), (0, 0), (0, CHUNK - DECAY_RANK), (0, 0)))
    w_out_b = w_out.astype(_BF16)
    w_s = w_spatial.astype(_BF16)

    row = lambda g: g.reshape(1, -1)
    for l in range(depth):
        uv, gate, v, qf, kf, qb, kb, cols = _inproj_call(stream, modtab[l], row(g_pre_mix[l]), w_main[l], w_a[l],
                                                         wd[l], b_decay[l][:, None, :])
        ya = _gmlp_call(uv, row(g_vnorm[l]), w_s[l], b_spatial[l][:, :, None])
        o_f, o_b = _gla_call(qf, kf, qb, kb, v, cols, n_batch, ctx_len // CHUNK, seq // CHUNK)
        routed = l % 2 == 1
        outs = _outproj_call(ya, o_f, o_b, gate, stream, modtab[l], row(g_gla_norm[l]), row(g_post_mix[l]),
                             row(g_pre_ffn[l]), w_out_b[l], w_router[l // 2].T if routed else None)
        if routed:
            stream, f, route = outs
            stream = _moe_call(f, route, stream, modtab[l], row(g_post_ffn[l]),
                               w_exp_gate[l // 2].astype(_BF16), w_exp_up[l // 2].astype(_BF16),
                               w_exp_down[l // 2].astype(_BF16))
        else:
            stream, f = outs
            stream = _ffn_call(f, w_ffn_gate[l // 2].astype(_BF16), w_ffn_up[l // 2].astype(_BF16),
                               w_ffn_down[l // 2].astype(_BF16), stream, modtab[l], row(g_post_ffn[l]))
    return stream[n_batch * ctx_len:].reshape(n_batch, seq, d)
```

```python
import functools

import jax
import jax.numpy as jnp
from jax import lax
from jax.experimental import pallas as pl
from jax.experimental.pallas import tpu as pltpu

_F32 = jnp.float32
_BF16 = jnp.bfloat16

EPS = 1e-6
CHUNK = 128
MOD_BLOCK = 256
A_HEADS = 8
A_HEAD_DIM = 128
B_HEADS = 4
B_DK = 128
B_DV = 256
DECAY_RANK = 16
DECAY_TAU = 16.0
N_EXPERTS = 8
ROUTE_ROWS = 8
TOKEN_TILE = 512
FFN_TOKEN_TILE = 768
FF_TILE = 256
EXPERT_ROWS = 1024
EXPERT_SUB_ROWS = 512
COMBINE_ROWS = 256
VMEM_LIMIT = 56 << 20


def _dot(a, b):
    return jnp.dot(a, b, preferred_element_type=_F32)


def _dot_nt(a, b):
    return lax.dot_general(a, b, (((1,), (1,)), ((), ())), preferred_element_type=_F32)


def _dot_tn(a, b):
    return lax.dot_general(a, b, (((0,), (0,)), ((), ())), preferred_element_type=_F32)


def _split(x):
    hi = x.astype(_BF16)
    lo = (x - hi.astype(_F32)).astype(_BF16)
    return hi, lo


def _dot_f32(a, b, dot=_dot):
    ah, al = _split(a)
    bh, bl = _split(b)
    return dot(ah, bh) + dot(ah, bl) + dot(al, bh)


def _rms(x, g):
    return x * lax.rsqrt(jnp.mean(x * x, axis=-1, keepdims=True) + EPS) * g


def _gelu(x):
    return 0.5 * x * (1.0 + lax.erf(x * (2.0 ** -0.5)))


def _silu(x):
    return x * jax.nn.sigmoid(x)


def _log_sigmoid(z):
    return jnp.minimum(z, 0.0) - jnp.log1p(jnp.exp(-jnp.abs(z)))


def _params(semantics):
    return pltpu.CompilerParams(dimension_semantics=semantics, vmem_limit_bytes=VMEM_LIMIT)


def _resident(shape):
    zeros = (0,) * len(shape)
    return pl.BlockSpec(shape, lambda *_: zeros, pipeline_mode=pl.Buffered(1))


def _mod_kernel(c_ref, w_ref, b_ref, o_ref):
    s = _silu(c_ref[...]).astype(_BF16)
    o_ref[0] = _dot(s, w_ref[0].astype(_BF16)) + b_ref[0]


def _mod_call(cvec, w_mod, b_mod):
    depth, d, n = w_mod.shape
    tn = 1024
    return pl.pallas_call(
        _mod_kernel,
        grid=(depth, n // tn),
        in_specs=[
            pl.BlockSpec((ROUTE_ROWS, d), lambda l, j: (0, 0)),
            pl.BlockSpec((1, d, tn), lambda l, j: (l, 0, j)),
            pl.BlockSpec((1, 1, tn), lambda l, j: (l, 0, j)),
        ],
        out_specs=pl.BlockSpec((1, ROUTE_ROWS, tn), lambda l, j: (l, 0, j)),
        out_shape=jax.ShapeDtypeStruct((depth, ROUTE_ROWS, n), _F32),
        compiler_params=_params(("parallel", "parallel")),
        name="adaln_mod",
    )(cvec, w_mod, b_mod.reshape(depth, 1, n))


def _scan_masks(direction):
    row = lax.broadcasted_iota(jnp.int32, (CHUNK, CHUNK), 0)
    col = lax.broadcasted_iota(jnp.int32, (CHUNK, CHUNK), 1)
    return ((col <= row), CHUNK - 1) if direction == 0 else ((col >= row), 0)


def _inproj_kernel(x_ref, mt_ref, g_ref, w_ref, wa_ref, wd_ref, bd_ref,
                   uv_ref, gate_ref, v_ref, qf_ref, kf_ref, qb_ref, kb_ref, cols_ref, h_scr):
    tm = x_ref.shape[0]
    a_w = A_HEADS * A_HEAD_DIM
    k_w = B_HEADS * B_DK
    v_w = B_HEADS * B_DV
    for s in range(tm // MOD_BLOCK):
        rows = slice(s * MOD_BLOCK, (s + 1) * MOD_BLOCK)
        mt = mt_ref[s]
        h = _rms(x_ref[rows, :], g_ref[...]) * (1.0 + mt[1:2]) + mt[0:1]
        h_scr[rows, :] = h.astype(_BF16)
    h = h_scr[...]
    off_q = 2 * a_w
    off_gate = off_q + k_w
    off_k = off_gate + v_w
    off_v = off_k + k_w
    q = _dot(h, w_ref[:, off_q:off_q + k_w]) * (B_DK ** -0.5)
    k = _dot(h, w_ref[:, off_k:off_k + k_w])
    a = _dot(h, wa_ref[...])
    sub = lax.broadcasted_iota(jnp.int32, (CHUNK, k_w), 0)
    factor_rows = [jnp.zeros((CHUNK, k_w), _F32) for _ in range(tm // CHUNK)]
    for d, (q_ref, k_ref) in enumerate(((qf_ref, kf_ref), (qb_ref, kb_ref))):
        scanned, end = _scan_masks(d)
        tri = jnp.where(scanned, 1.0, 0.0).astype(_BF16)
        la = _log_sigmoid(_dot_f32(a[:, d * CHUNK:(d + 1) * CHUNK], wd_ref[d]) + bd_ref[d]) * (1.0 / DECAY_TAU)
        la_hi, la_lo = _split(la)
        for c in range(tm // CHUNK):
            rows = slice(c * CHUNK, (c + 1) * CHUNK)
            b = _dot(tri, la_hi[rows]) + _dot(tri, la_lo[rows])
            b_mid = b[CHUNK // 2:CHUNK // 2 + 1]
            b_end = b[end:end + 1]
            q_ref[rows, :] = (q[rows] * jnp.exp(b - b_mid)).astype(_BF16)
            k_ref[rows, :] = (k[rows] * jnp.exp(b_mid - b)).astype(_BF16)
            for r, factor in enumerate((jnp.exp(b_mid), jnp.exp(b_end - b_mid), jnp.exp(b_end))):
                factor_rows[c] = jnp.where(sub == 3 * d + r, factor, factor_rows[c])
    for c in range(tm // CHUNK):
        for hd in range(B_HEADS):
            kc = slice(hd * B_DK, (hd + 1) * B_DK)
            cols_ref[c, kc, :] = factor_rows[c][:, kc].T
    for c0 in range(0, 2 * a_w, a_w):
        uv_ref[:, c0:c0 + a_w] = _dot(h, w_ref[:, c0:c0 + a_w]).astype(_BF16)
    gate_ref[...] = _dot(h, w_ref[:, off_gate:off_gate + v_w]).astype(_BF16)
    v_ref[...] = _dot(h, w_ref[:, off_v:off_v + v_w]).astype(_BF16)


def _inproj_call(x, modtab, g_pre, w_main, w_a, wd, bd):
    n, d = x.shape
    tm = TOKEN_TILE
    a_w = A_HEADS * A_HEAD_DIM
    k_w = B_HEADS * B_DK
    v_w = B_HEADS * B_DV
    widths = (2 * a_w, v_w, v_w, k_w, k_w, k_w, k_w)
    tok = lambda w: pl.BlockSpec((tm, w), lambda i: (i, 0))
    return pl.pallas_call(
        _inproj_kernel,
        grid=(n // tm,),
        in_specs=[
            tok(d),
            pl.BlockSpec((tm // MOD_BLOCK, 8, d), lambda i: (i, 0, 0)),
            pl.BlockSpec((1, d), lambda i: (0, 0)),
            _resident(w_main.shape),
            _resident(w_a.shape),
            pl.BlockSpec(wd.shape, lambda i: (0, 0, 0)),
            pl.BlockSpec(bd.shape, lambda i: (0, 0, 0)),
        ],
        out_specs=[tok(w) for w in widths] + [pl.BlockSpec((tm // CHUNK, k_w, CHUNK), lambda i: (i, 0, 0))],
        out_shape=[jax.ShapeDtypeStruct((n, w), _BF16) for w in widths]
        + [jax.ShapeDtypeStruct((n // CHUNK, k_w, CHUNK), _F32)],
        scratch_shapes=[pltpu.VMEM((tm, d), _BF16)],
        compiler_params=_params(("parallel",)),
        name="in_proj",
    )(x, modtab, g_pre, w_main, w_a, wd, bd)


def _gmlp_kernel(uv_ref, gv_ref, ws_ref, bs_ref, y_ref):
    tm = uv_ref.shape[0]
    a_w = A_HEADS * A_HEAD_DIM
    for c in range(tm // CHUNK):
        rows = slice(c * CHUNK, (c + 1) * CHUNK)
        u = _gelu(uv_ref[rows, :a_w].astype(_F32))
        v = _rms(_gelu(uv_ref[rows, a_w:].astype(_F32)), gv_ref[...]).astype(_BF16)
        for h in range(A_HEADS):
            cols = slice(h * A_HEAD_DIM, (h + 1) * A_HEAD_DIM)
            s = _dot(ws_ref[h], v[:, cols]) + bs_ref[h]
            y_ref[rows, cols] = (u[:, cols] * s).astype(_BF16)


def _gmlp_call(uv, g_v, w_s, b_s):
    n = uv.shape[0]
    tm = TOKEN_TILE
    a_w = A_HEADS * A_HEAD_DIM
    return pl.pallas_call(
        _gmlp_kernel,
        grid=(n // tm,),
        in_specs=[
            pl.BlockSpec((tm, 2 * a_w), lambda i: (i, 0)),
            pl.BlockSpec((1, a_w), lambda i: (0, 0)),
            pl.BlockSpec((A_HEADS, CHUNK, CHUNK), lambda i: (0, 0, 0)),
            pl.BlockSpec((A_HEADS, CHUNK, 1), lambda i: (0, 0, 0)),
        ],
        out_specs=pl.BlockSpec((tm, a_w), lambda i: (i, 0)),
        out_shape=jax.ShapeDtypeStruct((n, a_w), _BF16),
        compiler_params=_params(("parallel",)),
        name="gmlp",
    )(uv, g_v, w_s, b_s)


def _gla_kernel(qf_ref, kf_ref, vf_ref, cf_ref, qb_ref, kb_ref, vb_ref, cb_ref, of_ref, ob_ref, s_ref):
    @pl.when(pl.program_id(1) == 0)
    def _():
        s_ref[...] = jnp.zeros_like(s_ref)

    streams = ((qf_ref, kf_ref, vf_ref, cf_ref, of_ref), (qb_ref, kb_ref, vb_ref, cb_ref, ob_ref))
    for d, (q_ref, k_ref, v_ref, c_ref, o_ref) in enumerate(streams):
        scanned, _ = _scan_masks(d)
        for h in range(B_HEADS):
            kc = slice(h * B_DK, (h + 1) * B_DK)
            vc = slice(h * B_DV, (h + 1) * B_DV)
            cols = c_ref[0, kc, :]
            e_mid, e_end, decay = (cols[:, 3 * d + r:3 * d + r + 1] for r in range(3))
            qd = q_ref[:, kc]
            kd = k_ref[:, kc]
            vh = v_ref[:, vc]
            state = s_ref[d, h]
            scores = jnp.where(scanned, _dot_nt(qd, kd), 0.0).astype(_BF16)
            o_ref[:, vc] = _dot(scores, vh) + _dot(qd, (e_mid * state).astype(_BF16))
            s_ref[d, h] = decay * state + e_end * _dot_tn(kd, vh)


def _gla_call(qf, kf, qb, kb, v, cols, n_batch, ctx_chunks, lat_chunks):
    n = v.shape[0]
    k_w = B_HEADS * B_DK
    v_w = B_HEADS * B_DV
    steps = ctx_chunks + lat_chunks
    lat0 = n_batch * ctx_chunks

    def fwd(b, t):
        return jnp.where(t < ctx_chunks, b * ctx_chunks + t, lat0 + b * lat_chunks + (t - ctx_chunks))

    def bwd(b, t):
        return jnp.where(t < ctx_chunks, b * ctx_chunks + (ctx_chunks - 1 - t),
                         lat0 + b * lat_chunks + (steps - 1 - t))

    def specs(chunk_of):
        tok = lambda w: pl.BlockSpec((CHUNK, w), lambda b, t: (chunk_of(b, t), 0))
        return [tok(k_w), tok(k_w), tok(v_w), pl.BlockSpec((1, k_w, CHUNK), lambda b, t: (chunk_of(b, t), 0, 0))]

    return pl.pallas_call(
        _gla_kernel,
        grid=(n_batch, steps),
        in_specs=specs(fwd) + specs(bwd),
        out_specs=[pl.BlockSpec((CHUNK, v_w), lambda b, t: (fwd(b, t), 0)),
                   pl.BlockSpec((CHUNK, v_w), lambda b, t: (bwd(b, t), 0))],
        out_shape=[jax.ShapeDtypeStruct((n, v_w), _F32)] * 2,
        scratch_shapes=[pltpu.VMEM((2, B_HEADS, B_DK, B_DV), _F32)],
        compiler_params=_params(("parallel", "arbitrary")),
        name="gla_scan",
    )(qf, kf, v, cols, qb, kb, v, cols)


def _outproj_kernel(*refs, routed):
    if routed:
        (ya_ref, of_ref, ob_ref, gate_ref, x_ref, mt_ref, gn_ref, gpost_ref, gffn_ref, w_ref, wr_ref,
         xo_ref, f_ref, route_ref, y_scr) = refs
    else:
        (ya_ref, of_ref, ob_ref, gate_ref, x_ref, mt_ref, gn_ref, gpost_ref, gffn_ref, w_ref,
         xo_ref, f_ref, y_scr) = refs
    a_w = A_HEADS * A_HEAD_DIM
    tm = x_ref.shape[0]
    for s in range(tm // MOD_BLOCK):
        rows = slice(s * MOD_BLOCK, (s + 1) * MOD_BLOCK)
        y_scr[rows, :a_w] = ya_ref[rows, :]
        for h in range(B_HEADS):
            vc = slice(h * B_DV, (h + 1) * B_DV)
            o = of_ref[rows, vc] + ob_ref[rows, vc]
            o = o * lax.rsqrt(jnp.mean(o * o, axis=-1, keepdims=True) + EPS) * gn_ref[:, vc]
            y_scr[rows, a_w + h * B_DV:a_w + (h + 1) * B_DV] = (
                o * _silu(gate_ref[rows, vc].astype(_F32))).astype(_BF16)
        m = _rms(_dot(y_scr[rows, :], w_ref[...]), gpost_ref[...])
        mt = mt_ref[s]
        x = x_ref[rows, :] + mt[2:3] * m
        xo_ref[rows, :] = x
        f = _rms(x, gffn_ref[...]) * (1.0 + mt[4:5]) + mt[3:4]
        f_ref[rows, :] = f.astype(f_ref.dtype)
    if routed:
        logits = _dot_f32(wr_ref[...], f_ref[...], _dot_nt)
        eidx = lax.broadcasted_iota(jnp.int32, logits.shape, 0).astype(_F32)
        none = float(N_EXPERTS)
        m1 = jnp.max(logits, axis=0, keepdims=True)
        i1 = jnp.min(jnp.where(logits == m1, eidx, none), axis=0, keepdims=True)
        rest = jnp.where(eidx == i1, -jnp.inf, logits)
        m2 = jnp.max(rest, axis=0, keepdims=True)
        i2 = jnp.min(jnp.where(rest == m2, eidx, none), axis=0, keepdims=True)
        e2 = jnp.exp(m2 - m1)
        w1 = 1.0 / (1.0 + e2)
        w2 = e2 / (1.0 + e2)
        r = lax.broadcasted_iota(jnp.int32, logits.shape, 0)
        route_ref[...] = jnp.where(r == 0, i1, jnp.where(r == 1, i2, jnp.where(r == 2, w1, jnp.where(r == 3, w2, 0.0))))


def _outproj_call(ya, o_f, o_b, gate, x, modtab, g_norm, g_post, g_ffn, w_out, wr_t):
    n, d = x.shape
    tm = TOKEN_TILE
    routed = wr_t is not None
    tok = lambda w: pl.BlockSpec((tm, w), lambda i: (i, 0))
    vec = lambda w: pl.BlockSpec((1, w), lambda i: (0, 0))
    a_w = A_HEADS * A_HEAD_DIM
    v_w = B_HEADS * B_DV
    in_specs = [tok(a_w), tok(v_w), tok(v_w), tok(v_w), tok(d),
                pl.BlockSpec((tm // MOD_BLOCK, 8, d), lambda i: (i, 0, 0)),
                vec(v_w), vec(d), vec(d), _resident(w_out.shape)]
    args = [ya, o_f, o_b, gate, x, modtab, g_norm, g_post, g_ffn, w_out]
    out_specs = [tok(d), tok(d)]
    out_shape = [jax.ShapeDtypeStruct((n, d), _F32), jax.ShapeDtypeStruct((n, d), _F32 if routed else _BF16)]
    if routed:
        in_specs.append(pl.BlockSpec(wr_t.shape, lambda i: (0, 0)))
        args.append(wr_t)
        out_specs.append(pl.BlockSpec((ROUTE_ROWS, tm), lambda i: (0, i)))
        out_shape.append(jax.ShapeDtypeStruct((ROUTE_ROWS, n), _F32))
    return pl.pallas_call(
        functools.partial(_outproj_kernel, routed=routed),
        grid=(n // tm,),
        in_specs=in_specs,
        out_specs=out_specs,
        out_shape=out_shape,
        scratch_shapes=[pltpu.VMEM((tm, a_w + v_w), _BF16)],
        compiler_params=_params(("parallel",)),
        name="out_proj_routed" if routed else "out_proj",
    )(*args)


def _ffn_kernel(f_ref, wg_ref, wu_ref, wd_ref, x_ref, mt_ref, g_ref, o_ref):
    j = pl.program_id(1)

    @pl.when(j == 0)
    def _():
        o_ref[...] = jnp.zeros_like(o_ref)

    f = f_ref[...]
    act = (_silu(_dot(f, wg_ref[...].astype(_BF16))) * _dot(f, wu_ref[...].astype(_BF16))).astype(_BF16)
    o_ref[...] += _dot(act, wd_ref[...].astype(_BF16))

    @pl.when(j == pl.num_programs(1) - 1)
    def _():
        for s in range(o_ref.shape[0] // MOD_BLOCK):
            rows = slice(s * MOD_BLOCK, (s + 1) * MOD_BLOCK)
            o_ref[rows, :] = x_ref[rows, :] + mt_ref[s][5:6] * _rms(o_ref[rows, :], g_ref[...])


def _token_tile(n, target):
    tm = target - target % MOD_BLOCK
    while n % tm:
        tm -= MOD_BLOCK
    return tm


def _ffn_call(f, w_gate, w_up, w_down, x, modtab, g_post):
    n, d = x.shape
    d_ff = w_gate.shape[1]
    tm = _token_tile(n, FFN_TOKEN_TILE)
    tf = FF_TILE
    return pl.pallas_call(
        _ffn_kernel,
        grid=(n // tm, d_ff // tf),
        in_specs=[
            pl.BlockSpec((tm, d), lambda i, j: (i, 0)),
            pl.BlockSpec((d, tf), lambda i, j: (0, j)),
            pl.BlockSpec((d, tf), lambda i, j: (0, j)),
            pl.BlockSpec((tf, d), lambda i, j: (j, 0)),
            pl.BlockSpec((tm, d), lambda i, j: (i, 0)),
            pl.BlockSpec((tm // MOD_BLOCK, 8, d), lambda i, j: (i, 0, 0)),
            pl.BlockSpec((1, d), lambda i, j: (0, 0)),
        ],
        out_specs=pl.BlockSpec((tm, d), lambda i, j: (i, 0)),
        out_shape=jax.ShapeDtypeStruct((n, d), _F32),
        compiler_params=_params(("parallel", "arbitrary")),
        name="dense_ffn",
    )(f, w_gate, w_up, w_down, x, modtab, g_post)


def _rank_kernel(route_ref, rank_ref, count_ref, carry):
    @pl.when(pl.program_id(0) == 0)
    def _():
        carry[...] = jnp.zeros_like(carry)

    tl = route_ref.shape[1]
    r = route_ref[...]
    eidx = lax.broadcasted_iota(jnp.int32, r.shape, 0).astype(_F32)
    oh1 = jnp.where(eidx == r[0:1], 1.0, 0.0)
    oh2 = jnp.where(eidx == r[1:2], 1.0, 0.0)
    both = oh1 + oh2
    before = lax.broadcasted_iota(jnp.int32, (tl, tl), 0) < lax.broadcasted_iota(jnp.int32, (tl, tl), 1)
    seen = carry[:, 0:1] + _dot(both.astype(_BF16), jnp.where(before, 1.0, 0.0).astype(_BF16))
    rank1 = jnp.sum(oh1 * seen, axis=0, keepdims=True)
    rank2 = jnp.sum(oh2 * seen, axis=0, keepdims=True)
    row = lax.broadcasted_iota(jnp.int32, r.shape, 0)
    rank_ref[...] = jnp.where(row == 0, rank1, jnp.where(row == 1, rank2, 0.0))
    carry[...] = carry[...] + jnp.sum(both, axis=1, keepdims=True)
    count_ref[...] = carry[...]


def _rank_call(route):
    n = route.shape[1]
    tl = TOKEN_TILE
    return pl.pallas_call(
        _rank_kernel,
        grid=(n // tl,),
        in_specs=[pl.BlockSpec((ROUTE_ROWS, tl), lambda i: (0, i))],
        out_specs=[pl.BlockSpec((ROUTE_ROWS, tl), lambda i: (0, i)),
                   pl.BlockSpec((ROUTE_ROWS, 128), lambda i: (0, 0))],
        out_shape=[jax.ShapeDtypeStruct((ROUTE_ROWS, n), _F32),
                   jax.ShapeDtypeStruct((ROUTE_ROWS, 128), _F32)],
        scratch_shapes=[pltpu.VMEM((ROUTE_ROWS, 128), _F32)],
        compiler_params=_params(("arbitrary",)),
        name="moe_rank",
    )(route)


def _dispatch_kernel(dest_ref, f_ref, xs_in_ref, xs_ref, sem):
    del xs_in_ref
    td = f_ref.shape[0]

    def row_copy(t, k):
        return pltpu.make_async_copy(f_ref.at[pl.ds(t, 1)], xs_ref.at[pl.ds(dest_ref[0, k * td + t], 1)], sem)

    def issue(t, carry):
        row_copy(t, 0).start()
        row_copy(t, 1).start()
        return carry

    def drain(t, carry):
        row_copy(t, 0).wait()
        row_copy(t, 1).wait()
        return carry

    lax.fori_loop(0, td, issue, 0)
    lax.fori_loop(0, td, drain, 0)


def _dispatch_call(dest_blocks, f, xs_init):
    n_steps = dest_blocks.shape[0]
    return pl.pallas_call(
        _dispatch_kernel,
        grid=(n_steps,),
        in_specs=[
            pl.BlockSpec((None, 1, dest_blocks.shape[2]), lambda i: (i, 0, 0), memory_space=pltpu.SMEM),
            pl.BlockSpec((dest_blocks.shape[2] // 2, f.shape[1]), lambda i: (i, 0)),
            pl.BlockSpec(memory_space=pl.ANY),
        ],
        out_specs=pl.BlockSpec(memory_space=pl.ANY),
        out_shape=jax.ShapeDtypeStruct(xs_init.shape, xs_init.dtype),
        scratch_shapes=[pltpu.SemaphoreType.DMA(())],
        input_output_aliases={2: 0},
        compiler_params=_params(("arbitrary",)),
        name="moe_dispatch",
    )(dest_blocks, f, xs_init)


def _expert_kernel(be_ref, rows_ref, xs_ref, wg_ref, wu_ref, wd_ref, ys_ref, xb_scr, wg_scr, wu_scr, wd_scr):
    del be_ref
    n_rows = rows_ref[pl.program_id(0)]

    @pl.when(pl.program_id(1) == 0)
    def _():
        ys_ref[...] = jnp.zeros_like(ys_ref)
        xb_scr[...] = xs_ref[...].astype(_BF16)

    @pl.when(n_rows > 0)
    def _():
        wg_scr[...] = wg_ref[...].astype(_BF16)
        wu_scr[...] = wu_ref[...].astype(_BF16)
        wd_scr[...] = wd_ref[...].astype(_BF16)

    for s in range(EXPERT_ROWS // EXPERT_SUB_ROWS):
        @pl.when(n_rows > s * EXPERT_SUB_ROWS)
        def _():
            rows = slice(s * EXPERT_SUB_ROWS, (s + 1) * EXPERT_SUB_ROWS)
            xb = xb_scr[rows, :]
            act = (_silu(_dot(xb, wg_scr[...])) * _dot(xb, wu_scr[...])).astype(_BF16)
            ys_ref[rows, :] += _dot(act, wd_scr[...])


def _expert_call(block_expert, block_rows, xs, w_gate, w_up, w_down):
    cap, d = xs.shape
    d_ff = w_gate.shape[2]
    tb = EXPERT_ROWS
    tf = FF_TILE
    n_ff = d_ff // tf

    def ff_of(i, j, rows):
        return jnp.where(rows[i] > 0, j, n_ff - 1)

    grid_spec = pltpu.PrefetchScalarGridSpec(
        num_scalar_prefetch=2,
        grid=(cap // tb, n_ff),
        in_specs=[
            pl.BlockSpec((tb, d), lambda i, j, be, rows: (i, 0)),
            pl.BlockSpec((None, d, tf), lambda i, j, be, rows: (be[i], 0, ff_of(i, j, rows))),
            pl.BlockSpec((None, d, tf), lambda i, j, be, rows: (be[i], 0, ff_of(i, j, rows))),
            pl.BlockSpec((None, tf, d), lambda i, j, be, rows: (be[i], ff_of(i, j, rows), 0)),
        ],
        out_specs=pl.BlockSpec((tb, d), lambda i, j, be, rows: (i, 0)),
        scratch_shapes=[pltpu.VMEM((tb, d), _BF16), pltpu.VMEM((d, tf), _BF16), pltpu.VMEM((d, tf), _BF16),
                        pltpu.VMEM((tf, d), _BF16)],
    )
    return pl.pallas_call(
        _expert_kernel,
        grid_spec=grid_spec,
        out_shape=jax.ShapeDtypeStruct((cap, d), _F32),
        compiler_params=_params(("parallel", "arbitrary")),
        name="moe_experts",
    )(block_expert, block_rows, xs, w_gate, w_up, w_down)


def _combine_kernel(dest_ref, ys_ref, w_ref, x_ref, mt_ref, g_ref, o_ref, buf, sem):
    tc = x_ref.shape[0]

    def row_copy(t, k):
        return pltpu.make_async_copy(ys_ref.at[pl.ds(dest_ref[0, k * tc + t], 1)], buf.at[k, pl.ds(t, 1)], sem)

    def issue(t, carry):
        row_copy(t, 0).start()
        row_copy(t, 1).start()
        return carry

    def drain(t, carry):
        row_copy(t, 0).wait()
        row_copy(t, 1).wait()
        return carry

    lax.fori_loop(0, tc, issue, 0)
    lax.fori_loop(0, tc, drain, 0)
    w = w_ref[...]
    f = w[:, 0:1] * buf[0] + w[:, 1:2] * buf[1]
    for s in range(tc // MOD_BLOCK):
        rows = slice(s * MOD_BLOCK, (s + 1) * MOD_BLOCK)
        o_ref[rows, :] = x_ref[rows, :] + mt_ref[s][5:6] * _rms(f[rows, :], g_ref[...])


def _combine_call(dest_blocks, ys, w_cols, x, modtab, g_post):
    n, d = x.shape
    tc = COMBINE_ROWS
    return pl.pallas_call(
        _combine_kernel,
        grid=(n // tc,),
        in_specs=[
            pl.BlockSpec((None, 1, 2 * tc), lambda i: (i, 0, 0), memory_space=pltpu.SMEM),
            pl.BlockSpec(memory_space=pl.ANY),
            pl.BlockSpec((tc, 2), lambda i: (i, 0)),
            pl.BlockSpec((tc, d), lambda i: (i, 0)),
            pl.BlockSpec((tc // MOD_BLOCK, 8, d), lambda i: (i, 0, 0)),
            pl.BlockSpec((1, d), lambda i: (0, 0)),
        ],
        out_specs=pl.BlockSpec((tc, d), lambda i: (i, 0)),
        out_shape=jax.ShapeDtypeStruct((n, d), _F32),
        scratch_shapes=[pltpu.VMEM((2, tc, d), _F32), pltpu.SemaphoreType.DMA(())],
        compiler_params=_params(("arbitrary",)),
        name="moe_combine",
    )(dest_blocks, ys, w_cols, x, modtab, g_post)


def _dest_blocks(dest, rows):
    n = dest.shape[1]
    return dest.reshape(2, n // rows, rows).transpose(1, 0, 2).reshape(n // rows, 1, 2 * rows)


def _moe_call(f, route, x, modtab, g_post, w_gate, w_up, w_down):
    n, d = x.shape
    tb = EXPERT_ROWS
    rank, counts = _rank_call(route)
    expert = route[0:2].astype(jnp.int32)
    counts = counts[:, 0].astype(jnp.int32)
    padded = (counts + tb - 1) // tb * tb
    pad_end = jnp.cumsum(padded)
    pad_start = pad_end - padded
    dest = rank[0:2].astype(jnp.int32) + sum(jnp.where(expert == e, pad_start[e], 0) for e in range(N_EXPERTS))
    n_blocks = -(-(2 * n + N_EXPERTS * (tb - 1)) // tb)
    block_start = jnp.arange(n_blocks, dtype=jnp.int32) * tb
    block_expert = jnp.minimum(jnp.sum(block_start[:, None] >= pad_end[None, :], axis=1), N_EXPERTS - 1).astype(jnp.int32)
    sub = EXPERT_SUB_ROWS
    used_end = pad_start + (counts + sub - 1) // sub * sub
    block_used_end = sum(jnp.where(block_expert == e, used_end[e], 0) for e in range(N_EXPERTS))
    block_rows = jnp.clip(block_used_end - block_start, 0, tb).astype(jnp.int32)
    xs = _dispatch_call(_dest_blocks(dest, TOKEN_TILE), f, jnp.zeros((n_blocks * tb, d), _F32))
    ys = _expert_call(block_expert, block_rows, xs, w_gate, w_up, w_down)
    return _combine_call(_dest_blocks(dest, COMBINE_ROWS), ys, route[2:4].T, x, modtab, g_post)


def kernel(x, c, ctx, c_ctx, w_mod, b_mod, g_pre_mix, g_post_mix, g_pre_ffn, g_post_ffn, w_in, w_out, g_vnorm, w_spatial, b_spatial, w_decay, b_decay, g_gla_norm, w_ffn_gate, w_ffn_up, w_ffn_down, w_router, w_exp_gate, w_exp_up, w_exp_down):
    n_batch, seq, d = x.shape
    ctx_len = ctx.shape[1]
    depth = w_mod.shape[0]
    a_w = A_HEADS * A_HEAD_DIM
    k_w = B_HEADS * B_DK
    v_w = B_HEADS * B_DV
    main_w = 2 * a_w + 2 * k_w + 2 * v_w
    assert w_in.shape[2] == main_w + 2 * DECAY_RANK and seq % TOKEN_TILE == 0
    assert (n_batch * ctx_len) % TOKEN_TILE == 0 and ctx_len % MOD_BLOCK == 0

    stream = jnp.concatenate([ctx.reshape(n_batch * ctx_len, d), x.reshape(n_batch * seq, d)], axis=0)

    cvec = jnp.zeros((ROUTE_ROWS, d), _F32).at[:n_batch].set(c).at[n_batch].set(c_ctx)
    mod = _mod_call(cvec, w_mod, b_mod).reshape(depth, ROUTE_ROWS, 6, d)
    per_block = lambda r, blocks: jnp.broadcast_to(mod[:, r:r + 1], (depth, blocks, 6, d))
    modtab = jnp.concatenate([per_block(n_batch, n_batch * ctx_len // MOD_BLOCK)]
                             + [per_block(b, seq // MOD_BLOCK) for b in range(n_batch)], axis=1)
    modtab = jnp.pad(modtab, ((0, 0), (0, 0), (0, 2), (0, 0)))

    w_main = w_in[:, :, :main_w].astype(_BF16)
    w_a = jnp.zeros((depth, d, 2 * CHUNK), _F32)
    w_a = w_a.at[:, :, :DECAY_RANK].set(w_in[:, :, main_w:main_w + DECAY_RANK])
    w_a = w_a.at[:, :, CHUNK:CHUNK + DECAY_RANK].set(w_in[:, :, main_w + DECAY_RANK:]).astype(_BF16)
    wd = jnp.pad(w_decay, ((0, 0), (0, 0), (0, CHUNK - DECAY_RANK), (0, 0)))
    w_out_b = w_out.astype(_BF16)
    w_s = w_spatial.astype(_BF16)

    row = lambda g: g.reshape(1, -1)
    for l in range(depth):
        uv, gate, v, qf, kf, qb, kb, cols = _inproj_call(stream, modtab[l], row(g_pre_mix[l]), w_main[l], w_a[l],
                                                         wd[l], b_decay[l][:, None, :])
        ya = _gmlp_call(uv, row(g_vnorm[l]), w_s[l], b_spatial[l][:, :, None])
        o_f, o_b = _gla_call(qf, kf, qb, kb, v, cols, n_batch, ctx_len // CHUNK, seq // CHUNK)
        routed = l % 2 == 1
        outs = _outproj_call(ya, o_f, o_b, gate, stream, modtab[l], row(g_gla_norm[l]), row(g_post_mix[l]),
                             row(g_pre_ffn[l]), w_out_b[l], w_router[l // 2].T if routed else None)
        if routed:
            stream, f, route = outs
            stream = _moe_call(f, route, stream, modtab[l], row(g_post_ffn[l]),
                               w_exp_gate[l // 2], w_exp_up[l // 2], w_exp_down[l // 2])
        else:
            stream, f = outs
            stream = _ffn_call(f, w_ffn_gate[l // 2], w_ffn_up[l // 2], w_ffn_down[l // 2], stream, modtab[l],
                               row(g_post_ffn[l]))
    return stream[n_batch * ctx_len:].reshape(n_batch, seq, d)
```

```python
import functools

import jax
import jax.numpy as jnp
from jax import lax
from jax.experimental import pallas as pl
from jax.experimental.pallas import tpu as pltpu

_F32 = jnp.float32
_BF16 = jnp.bfloat16

EPS = 1e-6
CHUNK = 128
GLA_GROUP = 2
MOD_BLOCK = 256
A_HEADS = 8
A_HEAD_DIM = 128
B_HEADS = 4
B_DK = 128
B_DV = 256
DECAY_RANK = 16
DECAY_TAU = 16.0
N_EXPERTS = 8
ROUTE_ROWS = 8
TOKEN_TILE = 512
FFN_TOKEN_TILE = 768
FF_TILE = 256
EXPERT_ROWS = 1024
EXPERT_SUB_ROWS = 512
COMBINE_ROWS = 256
VMEM_LIMIT = 56 << 20


def _dot(a, b):
    return jnp.dot(a, b, preferred_element_type=_F32)


def _dot_nt(a, b):
    return lax.dot_general(a, b, (((1,), (1,)), ((), ())), preferred_element_type=_F32)


def _dot_tn(a, b):
    return lax.dot_general(a, b, (((0,), (0,)), ((), ())), preferred_element_type=_F32)


def _split(x):
    hi = x.astype(_BF16)
    lo = (x - hi.astype(_F32)).astype(_BF16)
    return hi, lo


def _dot_f32(a, b, dot=_dot):
    ah, al = _split(a)
    bh, bl = _split(b)
    return dot(ah, bh) + dot(ah, bl) + dot(al, bh)


def _rms(x, g):
    return x * lax.rsqrt(jnp.mean(x * x, axis=-1, keepdims=True) + EPS) * g


def _gelu(x):
    return 0.5 * x * (1.0 + lax.erf(x * (2.0 ** -0.5)))


def _silu(x):
    return x * jax.nn.sigmoid(x)


def _log_sigmoid(z):
    return jnp.minimum(z, 0.0) - jnp.log1p(jnp.exp(-jnp.abs(z)))


def _params(semantics):
    return pltpu.CompilerParams(dimension_semantics=semantics, vmem_limit_bytes=VMEM_LIMIT)


def _layer_block(arr, layer, block=None, index=None, **kwargs):
    per_layer = arr.shape[1:]
    block = per_layer if block is None else block
    index = (lambda *_: (0,) * len(per_layer)) if index is None else index
    return pl.BlockSpec((None,) + tuple(block), lambda *a: (layer,) + tuple(index(*a)), **kwargs)


def _resident(arr, layer):
    return _layer_block(arr, layer, pipeline_mode=pl.Buffered(1))


def _mod_kernel(c_ref, w_ref, b_ref, o_ref):
    s = _silu(c_ref[...]).astype(_BF16)
    o_ref[0] = _dot(s, w_ref[0].astype(_BF16)) + b_ref[0]


def _mod_call(cvec, w_mod, b_mod):
    depth, d, n = w_mod.shape
    tn = 1024
    return pl.pallas_call(
        _mod_kernel,
        grid=(depth, n // tn),
        in_specs=[
            pl.BlockSpec((ROUTE_ROWS, d), lambda l, j: (0, 0)),
            pl.BlockSpec((1, d, tn), lambda l, j: (l, 0, j)),
            pl.BlockSpec((1, 1, tn), lambda l, j: (l, 0, j)),
        ],
        out_specs=pl.BlockSpec((1, ROUTE_ROWS, tn), lambda l, j: (l, 0, j)),
        out_shape=jax.ShapeDtypeStruct((depth, ROUTE_ROWS, n), _F32),
        compiler_params=_params(("parallel", "parallel")),
        name="adaln_mod",
    )(cvec, w_mod, b_mod.reshape(depth, 1, n))


def _scan_masks(direction):
    row = lax.broadcasted_iota(jnp.int32, (CHUNK, CHUNK), 0)
    col = lax.broadcasted_iota(jnp.int32, (CHUNK, CHUNK), 1)
    return ((col <= row), CHUNK - 1) if direction == 0 else ((col >= row), 0)


def _inproj_kernel(x_ref, mt_ref, g_ref, w_ref, wa_ref, wd_ref, bd_ref,
                   uv_ref, gate_ref, v_ref, qf_ref, kf_ref, qb_ref, kb_ref, cols_ref, h_scr):
    tm = x_ref.shape[0]
    a_w = A_HEADS * A_HEAD_DIM
    k_w = B_HEADS * B_DK
    v_w = B_HEADS * B_DV
    for s in range(tm // MOD_BLOCK):
        rows = slice(s * MOD_BLOCK, (s + 1) * MOD_BLOCK)
        mt = mt_ref[s]
        h = _rms(x_ref[rows, :], g_ref[...]) * (1.0 + mt[1:2]) + mt[0:1]
        h_scr[rows, :] = h.astype(_BF16)
    h = h_scr[...]
    off_q = 2 * a_w
    off_gate = off_q + k_w
    off_k = off_gate + v_w
    off_v = off_k + k_w
    q = _dot(h, w_ref[:, off_q:off_q + k_w]) * (B_DK ** -0.5)
    k = _dot(h, w_ref[:, off_k:off_k + k_w])
    a = _dot(h, wa_ref[...])
    sub = lax.broadcasted_iota(jnp.int32, (CHUNK, k_w), 0)
    factor_rows = [jnp.zeros((CHUNK, k_w), _F32) for _ in range(tm // CHUNK)]
    for d, (q_ref, k_ref) in enumerate(((qf_ref, kf_ref), (qb_ref, kb_ref))):
        scanned, end = _scan_masks(d)
        tri = jnp.where(scanned, 1.0, 0.0).astype(_BF16)
        la = _log_sigmoid(_dot_f32(a[:, d * CHUNK:(d + 1) * CHUNK], wd_ref[d]) + bd_ref[d]) * (1.0 / DECAY_TAU)
        la_hi, la_lo = _split(la)
        for c in range(tm // CHUNK):
            rows = slice(c * CHUNK, (c + 1) * CHUNK)
            b = _dot(tri, la_hi[rows]) + _dot(tri, la_lo[rows])
            b_mid = b[CHUNK // 2:CHUNK // 2 + 1]
            b_end = b[end:end + 1]
            q_ref[rows, :] = (q[rows] * jnp.exp(b - b_mid)).astype(_BF16)
            k_ref[rows, :] = (k[rows] * jnp.exp(b_mid - b)).astype(_BF16)
            for r, factor in enumerate((jnp.exp(b_mid), jnp.exp(b_end - b_mid), jnp.exp(b_end))):
                factor_rows[c] = jnp.where(sub == 3 * d + r, factor, factor_rows[c])
    for c in range(tm // CHUNK):
        for hd in range(B_HEADS):
            kc = slice(hd * B_DK, (hd + 1) * B_DK)
            cols_ref[c, kc, :] = factor_rows[c][:, kc].T
    for c0 in range(0, 2 * a_w, a_w):
        uv_ref[:, c0:c0 + a_w] = _dot(h, w_ref[:, c0:c0 + a_w]).astype(_BF16)
    gate_ref[...] = _dot(h, w_ref[:, off_gate:off_gate + v_w]).astype(_BF16)
    v_ref[...] = _dot(h, w_ref[:, off_v:off_v + v_w]).astype(_BF16)


def _inproj_call(layer, x, modtab, g_pre, w_main, w_a, wd, bd):
    n, d = x.shape
    tm = TOKEN_TILE
    a_w = A_HEADS * A_HEAD_DIM
    k_w = B_HEADS * B_DK
    v_w = B_HEADS * B_DV
    widths = (2 * a_w, v_w, v_w, k_w, k_w, k_w, k_w)
    tok = lambda w: pl.BlockSpec((tm, w), lambda i: (i, 0))
    return pl.pallas_call(
        _inproj_kernel,
        grid=(n // tm,),
        in_specs=[
            tok(d),
            _layer_block(modtab, layer, (tm // MOD_BLOCK, 8, d), lambda i: (i, 0, 0)),
            _layer_block(g_pre, layer),
            _resident(w_main, layer),
            _resident(w_a, layer),
            _layer_block(wd, layer),
            _layer_block(bd, layer),
        ],
        out_specs=[tok(w) for w in widths] + [pl.BlockSpec((tm // CHUNK, k_w, CHUNK), lambda i: (i, 0, 0))],
        out_shape=[jax.ShapeDtypeStruct((n, w), _BF16) for w in widths]
        + [jax.ShapeDtypeStruct((n // CHUNK, k_w, CHUNK), _F32)],
        scratch_shapes=[pltpu.VMEM((tm, d), _BF16)],
        compiler_params=_params(("parallel",)),
        name="in_proj",
    )(x, modtab, g_pre, w_main, w_a, wd, bd)


def _gmlp_kernel(uv_ref, gv_ref, ws_ref, bs_ref, y_ref):
    tm = uv_ref.shape[0]
    a_w = A_HEADS * A_HEAD_DIM
    for c in range(tm // CHUNK):
        rows = slice(c * CHUNK, (c + 1) * CHUNK)
        u = _gelu(uv_ref[rows, :a_w].astype(_F32))
        v = _rms(_gelu(uv_ref[rows, a_w:].astype(_F32)), gv_ref[...]).astype(_BF16)
        for h in range(A_HEADS):
            cols = slice(h * A_HEAD_DIM, (h + 1) * A_HEAD_DIM)
            s = _dot(ws_ref[h], v[:, cols]) + bs_ref[h]
            y_ref[rows, cols] = (u[:, cols] * s).astype(_BF16)


def _gmlp_call(layer, uv, g_v, w_s, b_s):
    n = uv.shape[0]
    tm = TOKEN_TILE
    a_w = A_HEADS * A_HEAD_DIM
    return pl.pallas_call(
        _gmlp_kernel,
        grid=(n // tm,),
        in_specs=[
            pl.BlockSpec((tm, 2 * a_w), lambda i: (i, 0)),
            _layer_block(g_v, layer),
            _layer_block(w_s, layer),
            _layer_block(b_s, layer),
        ],
        out_specs=pl.BlockSpec((tm, a_w), lambda i: (i, 0)),
        out_shape=jax.ShapeDtypeStruct((n, a_w), _BF16),
        compiler_params=_params(("parallel",)),
        name="gmlp",
    )(uv, g_v, w_s, b_s)


def _gla_kernel(qf_ref, kf_ref, vf_ref, cf_ref, qb_ref, kb_ref, vb_ref, cb_ref, of_ref, ob_ref, s_ref):
    @pl.when(pl.program_id(1) == 0)
    def _():
        s_ref[...] = jnp.zeros_like(s_ref)

    group = qf_ref.shape[0] // CHUNK
    streams = ((qf_ref, kf_ref, vf_ref, cf_ref, of_ref), (qb_ref, kb_ref, vb_ref, cb_ref, ob_ref))
    for g in range(group):
        for d, (q_ref, k_ref, v_ref, c_ref, o_ref) in enumerate(streams):
            scanned, _ = _scan_masks(d)
            c = g if d == 0 else group - 1 - g
            rows = slice(c * CHUNK, (c + 1) * CHUNK)
            for h in range(B_HEADS):
                kc = slice(h * B_DK, (h + 1) * B_DK)
                vc = slice(h * B_DV, (h + 1) * B_DV)
                cols = c_ref[c, kc, :]
                e_mid, e_end, decay = (cols[:, 3 * d + r:3 * d + r + 1] for r in range(3))
                qd = q_ref[rows, kc]
                kd = k_ref[rows, kc]
                vh = v_ref[rows, vc]
                state = s_ref[d, h]
                scores = jnp.where(scanned, _dot_nt(qd, kd), 0.0).astype(_BF16)
                o_ref[rows, vc] = _dot(scores, vh) + _dot(qd, (e_mid * state).astype(_BF16))
                s_ref[d, h] = decay * state + e_end * _dot_tn(kd, vh)


def _gla_call(qf, kf, qb, kb, v, cols, n_batch, ctx_chunks, lat_chunks):
    n = v.shape[0]
    k_w = B_HEADS * B_DK
    v_w = B_HEADS * B_DV
    group = GLA_GROUP
    assert ctx_chunks % group == 0 and lat_chunks % group == 0
    ctx_steps = ctx_chunks // group
    lat_steps = lat_chunks // group
    steps = ctx_steps + lat_steps
    lat0 = n_batch * ctx_steps

    def fwd(b, t):
        return jnp.where(t < ctx_steps, b * ctx_steps + t, lat0 + b * lat_steps + (t - ctx_steps))

    def bwd(b, t):
        return jnp.where(t < ctx_steps, b * ctx_steps + (ctx_steps - 1 - t),
                         lat0 + b * lat_steps + (steps - 1 - t))

    def specs(group_of):
        tok = lambda w: pl.BlockSpec((group * CHUNK, w), lambda b, t: (group_of(b, t), 0))
        return [tok(k_w), tok(k_w), tok(v_w),
                pl.BlockSpec((group, k_w, CHUNK), lambda b, t: (group_of(b, t), 0, 0))]

    return pl.pallas_call(
        _gla_kernel,
        grid=(n_batch, steps),
        in_specs=specs(fwd) + specs(bwd),
        out_specs=[pl.BlockSpec((group * CHUNK, v_w), lambda b, t: (fwd(b, t), 0)),
                   pl.BlockSpec((group * CHUNK, v_w), lambda b, t: (bwd(b, t), 0))],
        out_shape=[jax.ShapeDtypeStruct((n, v_w), _F32)] * 2,
        scratch_shapes=[pltpu.VMEM((2, B_HEADS, B_DK, B_DV), _F32)],
        compiler_params=_params(("parallel", "arbitrary")),
        name="gla_scan",
    )(qf, kf, v, cols, qb, kb, v, cols)


def _outproj_kernel(*refs, routed):
    if routed:
        (ya_ref, of_ref, ob_ref, gate_ref, x_ref, mt_ref, gn_ref, gpost_ref, gffn_ref, w_ref, wr_ref,
         xo_ref, f_ref, route_ref, y_scr) = refs
    else:
        (ya_ref, of_ref, ob_ref, gate_ref, x_ref, mt_ref, gn_ref, gpost_ref, gffn_ref, w_ref,
         xo_ref, f_ref, y_scr) = refs
    a_w = A_HEADS * A_HEAD_DIM
    tm = x_ref.shape[0]
    for s in range(tm // MOD_BLOCK):
        rows = slice(s * MOD_BLOCK, (s + 1) * MOD_BLOCK)
        y_scr[rows, :a_w] = ya_ref[rows, :]
        for h in range(B_HEADS):
            vc = slice(h * B_DV, (h + 1) * B_DV)
            o = of_ref[rows, vc] + ob_ref[rows, vc]
            o = o * lax.rsqrt(jnp.mean(o * o, axis=-1, keepdims=True) + EPS) * gn_ref[:, vc]
            y_scr[rows, a_w + h * B_DV:a_w + (h + 1) * B_DV] = (
                o * _silu(gate_ref[rows, vc].astype(_F32))).astype(_BF16)
        m = _rms(_dot(y_scr[rows, :], w_ref[...]), gpost_ref[...])
        mt = mt_ref[s]
        x = x_ref[rows, :] + mt[2:3] * m
        xo_ref[rows, :] = x
        f = _rms(x, gffn_ref[...]) * (1.0 + mt[4:5]) + mt[3:4]
        f_ref[rows, :] = f.astype(f_ref.dtype)
    if routed:
        logits = _dot_f32(wr_ref[...], f_ref[...], _dot_nt)
        eidx = lax.broadcasted_iota(jnp.int32, logits.shape, 0).astype(_F32)
        none = float(N_EXPERTS)
        m1 = jnp.max(logits, axis=0, keepdims=True)
        i1 = jnp.min(jnp.where(logits == m1, eidx, none), axis=0, keepdims=True)
        rest = jnp.where(eidx == i1, -jnp.inf, logits)
        m2 = jnp.max(rest, axis=0, keepdims=True)
        i2 = jnp.min(jnp.where(rest == m2, eidx, none), axis=0, keepdims=True)
        e2 = jnp.exp(m2 - m1)
        w1 = 1.0 / (1.0 + e2)
        w2 = e2 / (1.0 + e2)
        r = lax.broadcasted_iota(jnp.int32, logits.shape, 0)
        route_ref[...] = jnp.where(r == 0, i1, jnp.where(r == 1, i2, jnp.where(r == 2, w1, jnp.where(r == 3, w2, 0.0))))


def _outproj_call(layer, ya, o_f, o_b, gate, x, modtab, g_norm, g_post, g_ffn, w_out, wr_t, route_layer):
    n, d = x.shape
    tm = TOKEN_TILE
    routed = wr_t is not None
    tok = lambda w: pl.BlockSpec((tm, w), lambda i: (i, 0))
    a_w = A_HEADS * A_HEAD_DIM
    v_w = B_HEADS * B_DV
    in_specs = [tok(a_w), tok(v_w), tok(v_w), tok(v_w), tok(d),
                _layer_block(modtab, layer, (tm // MOD_BLOCK, 8, d), lambda i: (i, 0, 0)),
                _layer_block(g_norm, layer), _layer_block(g_post, layer), _layer_block(g_ffn, layer),
                _resident(w_out, layer)]
    args = [ya, o_f, o_b, gate, x, modtab, g_norm, g_post, g_ffn, w_out]
    out_specs = [tok(d), tok(d)]
    out_shape = [jax.ShapeDtypeStruct((n, d), _F32), jax.ShapeDtypeStruct((n, d), _F32 if routed else _BF16)]
    if routed:
        in_specs.append(_layer_block(wr_t, route_layer))
        args.append(wr_t)
        out_specs.append(pl.BlockSpec((ROUTE_ROWS, tm), lambda i: (0, i)))
        out_shape.append(jax.ShapeDtypeStruct((ROUTE_ROWS, n), _F32))
    return pl.pallas_call(
        functools.partial(_outproj_kernel, routed=routed),
        grid=(n // tm,),
        in_specs=in_specs,
        out_specs=out_specs,
        out_shape=out_shape,
        scratch_shapes=[pltpu.VMEM((tm, a_w + v_w), _BF16)],
        compiler_params=_params(("parallel",)),
        name="out_proj_routed" if routed else "out_proj",
    )(*args)


def _ffn_kernel(f_ref, wg_ref, wu_ref, wd_ref, x_ref, mt_ref, g_ref, o_ref):
    j = pl.program_id(1)

    @pl.when(j == 0)
    def _():
        o_ref[...] = jnp.zeros_like(o_ref)

    f = f_ref[...]
    act = (_silu(_dot(f, wg_ref[...].astype(_BF16))) * _dot(f, wu_ref[...].astype(_BF16))).astype(_BF16)
    o_ref[...] += _dot(act, wd_ref[...].astype(_BF16))

    @pl.when(j == pl.num_programs(1) - 1)
    def _():
        for s in range(o_ref.shape[0] // MOD_BLOCK):
            rows = slice(s * MOD_BLOCK, (s + 1) * MOD_BLOCK)
            o_ref[rows, :] = x_ref[rows, :] + mt_ref[s][5:6] * _rms(o_ref[rows, :], g_ref[...])


def _token_tile(n, target):
    tm = target - target % MOD_BLOCK
    while n % tm:
        tm -= MOD_BLOCK
    return tm


def _ffn_call(layer, ffn_layer, f, w_gate, w_up, w_down, x, modtab, g_post):
    n, d = x.shape
    d_ff = w_gate.shape[2]
    tm = _token_tile(n, FFN_TOKEN_TILE)
    tf = FF_TILE
    return pl.pallas_call(
        _ffn_kernel,
        grid=(n // tm, d_ff // tf),
        in_specs=[
            pl.BlockSpec((tm, d), lambda i, j: (i, 0)),
            _layer_block(w_gate, ffn_layer, (d, tf), lambda i, j: (0, j)),
            _layer_block(w_up, ffn_layer, (d, tf), lambda i, j: (0, j)),
            _layer_block(w_down, ffn_layer, (tf, d), lambda i, j: (j, 0)),
            pl.BlockSpec((tm, d), lambda i, j: (i, 0)),
            _layer_block(modtab, layer, (tm // MOD_BLOCK, 8, d), lambda i, j: (i, 0, 0)),
            _layer_block(g_post, layer),
        ],
        out_specs=pl.BlockSpec((tm, d), lambda i, j: (i, 0)),
        out_shape=jax.ShapeDtypeStruct((n, d), _F32),
        compiler_params=_params(("parallel", "arbitrary")),
        name="dense_ffn",
    )(f, w_gate, w_up, w_down, x, modtab, g_post)


def _rank_kernel(route_ref, rank_ref, count_ref, carry):
    @pl.when(pl.program_id(0) == 0)
    def _():
        carry[...] = jnp.zeros_like(carry)

    tl = route_ref.shape[1]
    r = route_ref[...]
    eidx = lax.broadcasted_iota(jnp.int32, r.shape, 0).astype(_F32)
    oh1 = jnp.where(eidx == r[0:1], 1.0, 0.0)
    oh2 = jnp.where(eidx == r[1:2], 1.0, 0.0)
    both = oh1 + oh2
    before = lax.broadcasted_iota(jnp.int32, (tl, tl), 0) < lax.broadcasted_iota(jnp.int32, (tl, tl), 1)
    seen = carry[:, 0:1] + _dot(both.astype(_BF16), jnp.where(before, 1.0, 0.0).astype(_BF16))
    rank1 = jnp.sum(oh1 * seen, axis=0, keepdims=True)
    rank2 = jnp.sum(oh2 * seen, axis=0, keepdims=True)
    row = lax.broadcasted_iota(jnp.int32, r.shape, 0)
    rank_ref[...] = jnp.where(row == 0, rank1, jnp.where(row == 1, rank2, 0.0))
    carry[...] = carry[...] + jnp.sum(both, axis=1, keepdims=True)
    count_ref[...] = carry[...]


def _rank_call(route):
    n = route.shape[1]
    tl = TOKEN_TILE
    return pl.pallas_call(
        _rank_kernel,
        grid=(n // tl,),
        in_specs=[pl.BlockSpec((ROUTE_ROWS, tl), lambda i: (0, i))],
        out_specs=[pl.BlockSpec((ROUTE_ROWS, tl), lambda i: (0, i)),
                   pl.BlockSpec((ROUTE_ROWS, 128), lambda i: (0, 0))],
        out_shape=[jax.ShapeDtypeStruct((ROUTE_ROWS, n), _F32),
                   jax.ShapeDtypeStruct((ROUTE_ROWS, 128), _F32)],
        scratch_shapes=[pltpu.VMEM((ROUTE_ROWS, 128), _F32)],
        compiler_params=_params(("arbitrary",)),
        name="moe_rank",
    )(route)


def _dispatch_kernel(dest_ref, f_ref, xs_in_ref, xs_ref, sem):
    del xs_in_ref
    td = f_ref.shape[0]

    def row_copy(t, k):
        return pltpu.make_async_copy(f_ref.at[pl.ds(t, 1)], xs_ref.at[pl.ds(dest_ref[0, k * td + t], 1)], sem)

    def issue(t, carry):
        row_copy(t, 0).start()
        row_copy(t, 1).start()
        return carry

    def drain(t, carry):
        row_copy(t, 0).wait()
        row_copy(t, 1).wait()
        return carry

    lax.fori_loop(0, td, issue, 0)
    lax.fori_loop(0, td, drain, 0)


def _dispatch_call(dest_blocks, f, xs_init):
    n_steps = dest_blocks.shape[0]
    return pl.pallas_call(
        _dispatch_kernel,
        grid=(n_steps,),
        in_specs=[
            pl.BlockSpec((None, 1, dest_blocks.shape[2]), lambda i: (i, 0, 0), memory_space=pltpu.SMEM),
            pl.BlockSpec((dest_blocks.shape[2] // 2, f.shape[1]), lambda i: (i, 0)),
            pl.BlockSpec(memory_space=pl.ANY),
        ],
        out_specs=pl.BlockSpec(memory_space=pl.ANY),
        out_shape=jax.ShapeDtypeStruct(xs_init.shape, xs_init.dtype),
        scratch_shapes=[pltpu.SemaphoreType.DMA(())],
        input_output_aliases={2: 0},
        compiler_params=_params(("arbitrary",)),
        name="moe_dispatch",
    )(dest_blocks, f, xs_init)


def _expert_kernel(be_ref, rows_ref, xs_ref, wg_ref, wu_ref, wd_ref, ys_ref, xb_scr, wg_scr, wu_scr, wd_scr):
    del be_ref
    n_rows = rows_ref[pl.program_id(0)]

    @pl.when(pl.program_id(1) == 0)
    def _():
        ys_ref[...] = jnp.zeros_like(ys_ref)
        xb_scr[...] = xs_ref[...].astype(_BF16)

    @pl.when(n_rows > 0)
    def _():
        wg_scr[...] = wg_ref[...].astype(_BF16)
        wu_scr[...] = wu_ref[...].astype(_BF16)
        wd_scr[...] = wd_ref[...].astype(_BF16)

    for s in range(EXPERT_ROWS // EXPERT_SUB_ROWS):
        @pl.when(n_rows > s * EXPERT_SUB_ROWS)
        def _():
            rows = slice(s * EXPERT_SUB_ROWS, (s + 1) * EXPERT_SUB_ROWS)
            xb = xb_scr[rows, :]
            act = (_silu(_dot(xb, wg_scr[...])) * _dot(xb, wu_scr[...])).astype(_BF16)
            ys_ref[rows, :] += _dot(act, wd_scr[...])


def _expert_call(moe_layer, block_expert, block_rows, xs, w_gate, w_up, w_down):
    cap, d = xs.shape
    d_ff = w_gate.shape[3]
    tb = EXPERT_ROWS
    tf = FF_TILE
    n_ff = d_ff // tf

    def ff_of(i, j, rows):
        return jnp.where(rows[i] > 0, j, n_ff - 1)

    grid_spec = pltpu.PrefetchScalarGridSpec(
        num_scalar_prefetch=2,
        grid=(cap // tb, n_ff),
        in_specs=[
            pl.BlockSpec((tb, d), lambda i, j, be, rows: (i, 0)),
            _layer_block(w_gate, moe_layer, (None, d, tf), lambda i, j, be, rows: (be[i], 0, ff_of(i, j, rows))),
            _layer_block(w_up, moe_layer, (None, d, tf), lambda i, j, be, rows: (be[i], 0, ff_of(i, j, rows))),
            _layer_block(w_down, moe_layer, (None, tf, d), lambda i, j, be, rows: (be[i], ff_of(i, j, rows), 0)),
        ],
        out_specs=pl.BlockSpec((tb, d), lambda i, j, be, rows: (i, 0)),
        scratch_shapes=[pltpu.VMEM((tb, d), _BF16), pltpu.VMEM((d, tf), _BF16), pltpu.VMEM((d, tf), _BF16),
                        pltpu.VMEM((tf, d), _BF16)],
    )
    return pl.pallas_call(
        _expert_kernel,
        grid_spec=grid_spec,
        out_shape=jax.ShapeDtypeStruct((cap, d), _F32),
        compiler_params=_params(("parallel", "arbitrary")),
        name="moe_experts",
    )(block_expert, block_rows, xs, w_gate, w_up, w_down)


def _combine_kernel(dest_ref, ys_ref, w_ref, x_ref, mt_ref, g_ref, o_ref, buf, sem, *, skip_blocks):
    tc = x_ref.shape[0]

    def row_copy(t, k):
        return pltpu.make_async_copy(ys_ref.at[pl.ds(dest_ref[0, k * tc + t], 1)], buf.at[k, pl.ds(t, 1)], sem)

    def issue(t, carry):
        row_copy(t, 0).start()
        row_copy(t, 1).start()
        return carry

    def drain(t, carry):
        row_copy(t, 0).wait()
        row_copy(t, 1).wait()
        return carry

    @pl.when(pl.program_id(0) >= skip_blocks)
    def _():
        lax.fori_loop(0, tc, issue, 0)
        lax.fori_loop(0, tc, drain, 0)
        w = w_ref[...]
        f = w[:, 0:1] * buf[0] + w[:, 1:2] * buf[1]
        for s in range(tc // MOD_BLOCK):
            rows = slice(s * MOD_BLOCK, (s + 1) * MOD_BLOCK)
            o_ref[rows, :] = x_ref[rows, :] + mt_ref[s][5:6] * _rms(f[rows, :], g_ref[...])


def _combine_call(layer, dest_blocks, ys, w_cols, x, modtab, g_post, skip_rows):
    n, d = x.shape
    tc = COMBINE_ROWS
    skip_blocks = skip_rows // tc
    return pl.pallas_call(
        functools.partial(_combine_kernel, skip_blocks=skip_blocks),
        grid=(n // tc,),
        in_specs=[
            pl.BlockSpec((None, 1, 2 * tc), lambda i: (i, 0, 0), memory_space=pltpu.SMEM),
            pl.BlockSpec(memory_space=pl.ANY),
            pl.BlockSpec((tc, 2), lambda i: (i, 0)),
            pl.BlockSpec((tc, d), lambda i: (i, 0)),
            _layer_block(modtab, layer, (tc // MOD_BLOCK, 8, d), lambda i: (i, 0, 0)),
            _layer_block(g_post, layer),
        ],
        out_specs=pl.BlockSpec((tc, d), lambda i: (jnp.maximum(i - skip_blocks, 0), 0)),
        out_shape=jax.ShapeDtypeStruct((n - skip_rows, d), _F32),
        scratch_shapes=[pltpu.VMEM((2, tc, d), _F32), pltpu.SemaphoreType.DMA(())],
        compiler_params=_params(("arbitrary",)),
        name="moe_combine",
    )(dest_blocks, ys, w_cols, x, modtab, g_post)


def _dest_blocks(dest, rows):
    n = dest.shape[1]
    return dest.reshape(2, n // rows, rows).transpose(1, 0, 2).reshape(n // rows, 1, 2 * rows)


def _moe_call(layer, moe_layer, f, route, x, modtab, g_post, w_gate, w_up, w_down, skip_rows):
    n, d = x.shape
    tb = EXPERT_ROWS
    rank, counts = _rank_call(route)
    expert = route[0:2].astype(jnp.int32)
    counts = counts[:, 0].astype(jnp.int32)
    padded = (counts + tb - 1) // tb * tb
    pad_end = jnp.cumsum(padded)
    pad_start = pad_end - padded
    dest = rank[0:2].astype(jnp.int32) + sum(jnp.where(expert == e, pad_start[e], 0) for e in range(N_EXPERTS))
    n_blocks = -(-(2 * n + N_EXPERTS * (tb - 1)) // tb)
    block_start = jnp.arange(n_blocks, dtype=jnp.int32) * tb
    block_expert = jnp.minimum(jnp.sum(block_start[:, None] >= pad_end[None, :], axis=1), N_EXPERTS - 1).astype(jnp.int32)
    sub = EXPERT_SUB_ROWS
    used_end = pad_start + (counts + sub - 1) // sub * sub
    block_used_end = sum(jnp.where(block_expert == e, used_end[e], 0) for e in range(N_EXPERTS))
    block_rows = jnp.clip(block_used_end - block_start, 0, tb).astype(jnp.int32)
    xs = _dispatch_call(_dest_blocks(dest, TOKEN_TILE), f, jnp.zeros((n_blocks * tb, d), _F32))
    ys = _expert_call(moe_layer, block_expert, block_rows, xs, w_gate, w_up, w_down)
    return _combine_call(layer, _dest_blocks(dest, COMBINE_ROWS), ys, route[2:4].T, x, modtab, g_post, skip_rows)


def kernel(x, c, ctx, c_ctx, w_mod, b_mod, g_pre_mix, g_post_mix, g_pre_ffn, g_post_ffn, w_in, w_out, g_vnorm, w_spatial, b_spatial, w_decay, b_decay, g_gla_norm, w_ffn_gate, w_ffn_up, w_ffn_down, w_router, w_exp_gate, w_exp_up, w_exp_down):
    n_batch, seq, d = x.shape
    ctx_len = ctx.shape[1]
    depth = w_mod.shape[0]
    a_w = A_HEADS * A_HEAD_DIM
    k_w = B_HEADS * B_DK
    v_w = B_HEADS * B_DV
    main_w = 2 * a_w + 2 * k_w + 2 * v_w
    assert w_in.shape[2] == main_w + 2 * DECAY_RANK and seq % TOKEN_TILE == 0
    assert (n_batch * ctx_len) % TOKEN_TILE == 0 and ctx_len % MOD_BLOCK == 0

    stream = jnp.concatenate([ctx.reshape(n_batch * ctx_len, d), x.reshape(n_batch * seq, d)], axis=0)

    cvec = jnp.zeros((ROUTE_ROWS, d), _F32).at[:n_batch].set(c).at[n_batch].set(c_ctx)
    mod = _mod_call(cvec, w_mod, b_mod).reshape(depth, ROUTE_ROWS, 6, d)
    per_block = lambda r, blocks: jnp.broadcast_to(mod[:, r:r + 1], (depth, blocks, 6, d))
    modtab = jnp.concatenate([per_block(n_batch, n_batch * ctx_len // MOD_BLOCK)]
                             + [per_block(b, seq // MOD_BLOCK) for b in range(n_batch)], axis=1)
    modtab = jnp.pad(modtab, ((0, 0), (0, 0), (0, 2), (0, 0)))

    w_main = w_in[:, :, :main_w].astype(_BF16)
    w_a = jnp.zeros((depth, d, 2 * CHUNK), _F32)
    w_a = w_a.at[:, :, :DECAY_RANK].set(w_in[:, :, main_w:main_w + DECAY_RANK])
    w_a = w_a.at[:, :, CHUNK:CHUNK + DECAY_RANK].set(w_in[:, :, main_w + DECAY_RANK:]).astype(_BF16)
    wd = jnp.pad(w_decay, ((0, 0), (0, 0), (0, CHUNK - DECAY_RANK), (0, 0)))
    bd = b_decay[:, :, None, :]
    w_out_b = w_out.astype(_BF16)
    w_s = w_spatial.astype(_BF16)
    b_s = b_spatial[:, :, :, None]
    wr_t = w_router.transpose(0, 2, 1)
    rows = lambda g: g[:, None, :]
    g_pre_mix, g_post_mix, g_pre_ffn, g_post_ffn, g_vnorm, g_gla_norm = map(
        rows, (g_pre_mix, g_post_mix, g_pre_ffn, g_post_ffn, g_vnorm, g_gla_norm))

    n_ctx = n_batch * ctx_len
    for l in range(depth):
        last = l == depth - 1
        uv, gate, v, qf, kf, qb, kb, cols = _inproj_call(l, stream, modtab, g_pre_mix, w_main, w_a, wd, bd)
        ya = _gmlp_call(l, uv, g_vnorm, w_s, b_s)
        o_f, o_b = _gla_call(qf, kf, qb, kb, v, cols, n_batch, ctx_len // CHUNK, seq // CHUNK)
        routed = l % 2 == 1
        outs = _outproj_call(l, ya, o_f, o_b, gate, stream, modtab, g_gla_norm, g_post_mix, g_pre_ffn, w_out_b,
                             wr_t if routed else None, l // 2)
        if routed:
            stream, f, route = outs
            stream = _moe_call(l, l // 2, f, route, stream, modtab, g_post_ffn, w_exp_gate, w_exp_up, w_exp_down,
                               n_ctx if last else 0)
        else:
            stream, f = outs
            stream = _ffn_call(l, l // 2, f, w_ffn_gate, w_ffn_up, w_ffn_down, stream, modtab, g_post_ffn)
            if last:
                stream = stream[n_ctx:]
    return stream.reshape(n_batch, seq, d)
```

```python
import functools

import jax
import jax.numpy as jnp
from jax import lax
from jax.experimental import pallas as pl
from jax.experimental.pallas import tpu as pltpu

_F32 = jnp.float32
_BF16 = jnp.bfloat16

EPS = 1e-6
CHUNK = 128
GLA_GROUP = 2
MOD_BLOCK = 256
A_HEADS = 8
A_HEAD_DIM = 128
B_HEADS = 4
B_DK = 128
B_DV = 256
DECAY_RANK = 16
DECAY_TAU = 16.0
N_EXPERTS = 8
ROUTE_ROWS = 8
TOKEN_TILE = 512
FFN_TOKEN_TILE = 768
FF_TILE = 256
EXPERT_ROWS = 1024
EXPERT_SUB_ROWS = 512
COMBINE_ROWS = 256
VMEM_LIMIT = 56 << 20


def _dot(a, b):
    return jnp.dot(a, b, preferred_element_type=_F32)


def _dot_nt(a, b):
    return lax.dot_general(a, b, (((1,), (1,)), ((), ())), preferred_element_type=_F32)


def _dot_tn(a, b):
    return lax.dot_general(a, b, (((0,), (0,)), ((), ())), preferred_element_type=_F32)


def _split(x):
    hi = x.astype(_BF16)
    lo = (x - hi.astype(_F32)).astype(_BF16)
    return hi, lo


def _dot_f32(a, b, dot=_dot):
    ah, al = _split(a)
    bh, bl = _split(b)
    return dot(ah, bh) + dot(ah, bl) + dot(al, bh)


def _rms(x, g):
    return x * lax.rsqrt(jnp.mean(x * x, axis=-1, keepdims=True) + EPS) * g


def _gelu(x):
    return 0.5 * x * (1.0 + lax.erf(x * (2.0 ** -0.5)))


def _silu(x):
    return x * jax.nn.sigmoid(x)


def _log_sigmoid(z):
    return jnp.minimum(z, 0.0) - jnp.log1p(jnp.exp(-jnp.abs(z)))


def _pack_bf16_pair(left, right):
    hi = lax.bitcast_convert_type(left.astype(_BF16).astype(_F32), jnp.uint32)
    lo = lax.bitcast_convert_type(right.astype(_BF16).astype(_F32), jnp.uint32)
    return hi | (lo >> 16)


def _unpack_bf16_pair(packed):
    left = lax.bitcast_convert_type(packed & jnp.uint32(0xFFFF0000), _F32).astype(_BF16)
    right = lax.bitcast_convert_type(packed << 16, _F32).astype(_BF16)
    return left, right


def _params(semantics):
    return pltpu.CompilerParams(dimension_semantics=semantics, vmem_limit_bytes=VMEM_LIMIT)


def _layer_block(arr, layer, block=None, index=None, **kwargs):
    per_layer = arr.shape[1:]
    block = per_layer if block is None else block
    index = (lambda *_: (0,) * len(per_layer)) if index is None else index
    return pl.BlockSpec((None,) + tuple(block), lambda *a: (layer,) + tuple(index(*a)), **kwargs)


def _stream_specs(stream, tm):
    if not isinstance(stream, tuple):
        return [pl.BlockSpec((tm, stream.shape[1]), lambda i: (i, 0))]
    ctx, lat = stream
    assert ctx.shape[0] == tm and lat.shape[0] % tm == 0
    return [pl.BlockSpec((tm, ctx.shape[1]), lambda i: (0, 0)),
            pl.BlockSpec((tm, lat.shape[1]), lambda i: (jnp.maximum(i - 1, 0), 0))]


def _stream_rows(x_refs, rows):
    if len(x_refs) == 1:
        return x_refs[0][rows, :]
    ctx_ref, lat_ref = x_refs
    return jnp.where(pl.program_id(0) == 0, ctx_ref[rows, :], lat_ref[rows, :])


def _stream_shape(stream):
    if not isinstance(stream, tuple):
        return stream.shape
    return stream[0].shape[0] + stream[1].shape[0], stream[0].shape[1]


def _stream_args(stream):
    return list(stream) if isinstance(stream, tuple) else [stream]


def _resident(arr, layer):
    return _layer_block(arr, layer, pipeline_mode=pl.Buffered(1))


def _mod_kernel(c_ref, w_ref, b_ref, o_ref):
    s = _silu(c_ref[...]).astype(_BF16)
    o_ref[0] = _dot(s, w_ref[0].astype(_BF16)) + b_ref[0]


def _mod_call(cvec, w_mod, b_mod):
    depth, d, n = w_mod.shape
    tn = 1024
    return pl.pallas_call(
        _mod_kernel,
        grid=(depth, n // tn),
        in_specs=[
            pl.BlockSpec((ROUTE_ROWS, d), lambda l, j: (0, 0)),
            pl.BlockSpec((1, d, tn), lambda l, j: (l, 0, j)),
            pl.BlockSpec((1, 1, tn), lambda l, j: (l, 0, j)),
        ],
        out_specs=pl.BlockSpec((1, ROUTE_ROWS, tn), lambda l, j: (l, 0, j)),
        out_shape=jax.ShapeDtypeStruct((depth, ROUTE_ROWS, n), _F32),
        compiler_params=_params(("parallel", "parallel")),
        name="adaln_mod",
    )(cvec, w_mod, b_mod.reshape(depth, 1, n))


def _scan_masks(direction):
    row = lax.broadcasted_iota(jnp.int32, (CHUNK, CHUNK), 0)
    col = lax.broadcasted_iota(jnp.int32, (CHUNK, CHUNK), 1)
    return ((col <= row), CHUNK - 1) if direction == 0 else ((col >= row), 0)


def _inproj_kernel(*refs, n_stream):
    x_refs = refs[:n_stream]
    (mt_ref, g_ref, w_ref, wa_ref, wd_ref, bd_ref,
     uv_ref, gate_ref, v_ref, qf_ref, kf_ref, qb_ref, kb_ref, cols_ref, h_scr) = refs[n_stream:]
    tm = x_refs[0].shape[0]
    a_w = A_HEADS * A_HEAD_DIM
    k_w = B_HEADS * B_DK
    v_w = B_HEADS * B_DV
    for s in range(tm // MOD_BLOCK):
        rows = slice(s * MOD_BLOCK, (s + 1) * MOD_BLOCK)
        mt = mt_ref[s]
        h = _rms(_stream_rows(x_refs, rows), g_ref[...]) * (1.0 + mt[1:2]) + mt[0:1]
        h_scr[rows, :] = h.astype(_BF16)
    h = h_scr[...]
    off_q = 2 * a_w
    off_gate = off_q + k_w
    off_k = off_gate + v_w
    off_v = off_k + k_w
    q = _dot(h, w_ref[:, off_q:off_q + k_w]) * (B_DK ** -0.5)
    k = _dot(h, w_ref[:, off_k:off_k + k_w])
    a = _dot(h, wa_ref[...])
    sub = lax.broadcasted_iota(jnp.int32, (CHUNK, k_w), 0)
    factor_rows = [jnp.zeros((CHUNK, k_w), _F32) for _ in range(tm // CHUNK)]
    for d, (q_ref, k_ref) in enumerate(((qf_ref, kf_ref), (qb_ref, kb_ref))):
        scanned, end = _scan_masks(d)
        tri = jnp.where(scanned, 1.0, 0.0).astype(_BF16)
        la = _log_sigmoid(_dot_f32(a, wd_ref[d]) + bd_ref[d]) * (1.0 / DECAY_TAU)
        la_hi, la_lo = _split(la)
        for c in range(tm // CHUNK):
            rows = slice(c * CHUNK, (c + 1) * CHUNK)
            b = _dot(tri, la_hi[rows]) + _dot(tri, la_lo[rows])
            b_mid = b[CHUNK // 2:CHUNK // 2 + 1]
            b_end = b[end:end + 1]
            q_ref[rows, :] = (q[rows] * jnp.exp(b - b_mid)).astype(_BF16)
            kd = k[rows] * jnp.exp(b_mid - b)
            for hd in range(B_HEADS):
                kc = slice(hd * B_DK, (hd + 1) * B_DK)
                k_ref[c, kc, :] = kd[:, kc].T.astype(_BF16)
            for r, factor in enumerate((jnp.exp(b_mid), jnp.exp(b_end - b_mid), jnp.exp(b_end))):
                factor_rows[c] = jnp.where(sub == 3 * d + r, factor, factor_rows[c])
    for c in range(tm // CHUNK):
        for hd in range(B_HEADS):
            kc = slice(hd * B_DK, (hd + 1) * B_DK)
            cols_ref[c, kc, :] = factor_rows[c][:, kc].T
    for c0 in range(0, 2 * a_w, a_w):
        uv_ref[:, c0:c0 + a_w] = _dot(h, w_ref[:, c0:c0 + a_w]).astype(_BF16)
    gate_ref[...] = _dot(h, w_ref[:, off_gate:off_gate + v_w]).astype(_BF16)
    v_ref[...] = _dot(h, w_ref[:, off_v:off_v + v_w]).astype(_BF16)


def _inproj_call(layer, x, modtab, g_pre, w_main, w_a, wd, bd):
    n, d = _stream_shape(x)
    tm = TOKEN_TILE
    a_w = A_HEADS * A_HEAD_DIM
    k_w = B_HEADS * B_DK
    v_w = B_HEADS * B_DV
    tok = lambda w: pl.BlockSpec((tm, w), lambda i: (i, 0))
    tok_out = lambda w: (tok(w), jax.ShapeDtypeStruct((n, w), _BF16))
    per_chunk = lambda dtype: (pl.BlockSpec((tm // CHUNK, k_w, CHUNK), lambda i: (i, 0, 0)),
                               jax.ShapeDtypeStruct((n // CHUNK, k_w, CHUNK), dtype))
    outs = [tok_out(2 * a_w), tok_out(v_w), tok_out(v_w), tok_out(k_w), per_chunk(_BF16), tok_out(k_w),
            per_chunk(_BF16), per_chunk(_F32)]
    stream_specs = _stream_specs(x, tm)
    return pl.pallas_call(
        functools.partial(_inproj_kernel, n_stream=len(stream_specs)),
        grid=(n // tm,),
        in_specs=stream_specs + [
            _layer_block(modtab, layer, (tm // MOD_BLOCK, 8, d), lambda i: (i, 0, 0)),
            _layer_block(g_pre, layer),
            _resident(w_main, layer),
            _resident(w_a, layer),
            _layer_block(wd, layer),
            _layer_block(bd, layer),
        ],
        out_specs=[spec for spec, _ in outs],
        out_shape=[shape for _, shape in outs],
        scratch_shapes=[pltpu.VMEM((tm, d), _BF16)],
        compiler_params=_params(("parallel",)),
        name="in_proj",
    )(*_stream_args(x), modtab, g_pre, w_main, w_a, wd, bd)


def _gmlp_kernel(uv_ref, gv_ref, ws_ref, bs_ref, y_ref):
    tm = uv_ref.shape[0]
    a_w = A_HEADS * A_HEAD_DIM
    for c in range(tm // CHUNK):
        rows = slice(c * CHUNK, (c + 1) * CHUNK)
        u = _gelu(uv_ref[rows, :a_w].astype(_F32))
        v = _rms(_gelu(uv_ref[rows, a_w:].astype(_F32)), gv_ref[...]).astype(_BF16)
        for h in range(A_HEADS):
            cols = slice(h * A_HEAD_DIM, (h + 1) * A_HEAD_DIM)
            s = _dot(ws_ref[h], v[:, cols]) + bs_ref[h]
            y_ref[rows, cols] = (u[:, cols] * s).astype(_BF16)


def _gmlp_call(layer, uv, g_v, w_s, b_s):
    n = uv.shape[0]
    tm = TOKEN_TILE
    a_w = A_HEADS * A_HEAD_DIM
    return pl.pallas_call(
        _gmlp_kernel,
        grid=(n // tm,),
        in_specs=[
            pl.BlockSpec((tm, 2 * a_w), lambda i: (i, 0)),
            _layer_block(g_v, layer),
            _layer_block(w_s, layer),
            _layer_block(b_s, layer),
        ],
        out_specs=pl.BlockSpec((tm, a_w), lambda i: (i, 0)),
        out_shape=jax.ShapeDtypeStruct((n, a_w), _BF16),
        compiler_params=_params(("parallel",)),
        name="gmlp",
    )(uv, g_v, w_s, b_s)


def _gla_kernel(qf_ref, kf_ref, vf_ref, cf_ref, qb_ref, kb_ref, vb_ref, cb_ref, of_ref, ob_ref, s_ref):
    @pl.when(pl.program_id(1) == 0)
    def _():
        s_ref[...] = jnp.zeros_like(s_ref)

    group = qf_ref.shape[0] // CHUNK
    streams = ((qf_ref, kf_ref, vf_ref, cf_ref, of_ref), (qb_ref, kb_ref, vb_ref, cb_ref, ob_ref))
    for g in range(group):
        for d, (q_ref, k_ref, v_ref, c_ref, o_ref) in enumerate(streams):
            scanned, _ = _scan_masks(d)
            c = g if d == 0 else group - 1 - g
            rows = slice(c * CHUNK, (c + 1) * CHUNK)
            for h in range(B_HEADS):
                kc = slice(h * B_DK, (h + 1) * B_DK)
                vc = slice(h * B_DV, (h + 1) * B_DV)
                cols = c_ref[c, kc, :]
                e_mid, e_end, decay = (cols[:, 3 * d + r:3 * d + r + 1] for r in range(3))
                qd = q_ref[rows, kc]
                kd_t = k_ref[c, kc, :]
                vh = v_ref[rows, vc]
                state = s_ref[d, h]
                scores = jnp.where(scanned, _dot(qd, kd_t), 0.0).astype(_BF16)
                o_ref[rows, vc] = _dot(scores, vh) + _dot(qd, (e_mid * state).astype(_BF16))
                s_ref[d, h] = decay * state + e_end * _dot(kd_t, vh)


def _gla_call(qf, kf, qb, kb, v, cols, n_batch, ctx_chunks, lat_chunks):
    n = v.shape[0]
    k_w = B_HEADS * B_DK
    v_w = B_HEADS * B_DV
    group = GLA_GROUP
    assert ctx_chunks % group == 0 and lat_chunks % group == 0
    ctx_steps = ctx_chunks // group
    lat_steps = lat_chunks // group
    steps = ctx_steps + lat_steps
    lat0 = n_batch * ctx_steps

    def fwd(b, t):
        return jnp.where(t < ctx_steps, b * ctx_steps + t, lat0 + b * lat_steps + (t - ctx_steps))

    def bwd(b, t):
        return jnp.where(t < ctx_steps, b * ctx_steps + (ctx_steps - 1 - t),
                         lat0 + b * lat_steps + (steps - 1 - t))

    def specs(group_of):
        tok = lambda w: pl.BlockSpec((group * CHUNK, w), lambda b, t: (group_of(b, t), 0))
        per_chunk = pl.BlockSpec((group, k_w, CHUNK), lambda b, t: (group_of(b, t), 0, 0))
        return [tok(k_w), per_chunk, tok(v_w), per_chunk]

    return pl.pallas_call(
        _gla_kernel,
        grid=(n_batch, steps),
        in_specs=specs(fwd) + specs(bwd),
        out_specs=[pl.BlockSpec((group * CHUNK, v_w), lambda b, t: (fwd(b, t), 0)),
                   pl.BlockSpec((group * CHUNK, v_w), lambda b, t: (bwd(b, t), 0))],
        out_shape=[jax.ShapeDtypeStruct((n, v_w), _F32)] * 2,
        scratch_shapes=[pltpu.VMEM((2, B_HEADS, B_DK, B_DV), _F32)],
        compiler_params=_params(("parallel", "arbitrary")),
        name="gla_scan",
    )(qf, kf, v, cols, qb, kb, v, cols)


def _outproj_kernel(*refs, routed, n_stream):
    x_refs, refs = refs[:n_stream], refs[n_stream:]
    if routed:
        (ya_ref, of_ref, ob_ref, gate_ref, mt_ref, gn_ref, gpost_ref, gffn_ref, w_ref, wr_ref,
         xo_ref, f_ref, route_ref, y_scr, f_scr) = refs
    else:
        (ya_ref, of_ref, ob_ref, gate_ref, mt_ref, gn_ref, gpost_ref, gffn_ref, w_ref,
         xo_ref, f_ref, y_scr) = refs
    a_w = A_HEADS * A_HEAD_DIM
    tm = x_refs[0].shape[0]
    for s in range(tm // MOD_BLOCK):
        rows = slice(s * MOD_BLOCK, (s + 1) * MOD_BLOCK)
        y_scr[rows, :a_w] = ya_ref[rows, :]
        for h in range(B_HEADS):
            vc = slice(h * B_DV, (h + 1) * B_DV)
            o = of_ref[rows, vc] + ob_ref[rows, vc]
            o = o * lax.rsqrt(jnp.mean(o * o, axis=-1, keepdims=True) + EPS) * gn_ref[:, vc]
            y_scr[rows, a_w + h * B_DV:a_w + (h + 1) * B_DV] = (
                o * _silu(gate_ref[rows, vc].astype(_F32))).astype(_BF16)
        m = _rms(_dot(y_scr[rows, :], w_ref[...]), gpost_ref[...])
        mt = mt_ref[s]
        x = _stream_rows(x_refs, rows) + mt[2:3] * m
        xo_ref[rows, :] = x
        f = _rms(x, gffn_ref[...]) * (1.0 + mt[4:5]) + mt[3:4]
        if routed:
            f_scr[rows, :] = f
            half = f.shape[1] // 2
            f_ref[rows, :] = _pack_bf16_pair(f[:, :half], f[:, half:])
        else:
            f_ref[rows, :] = f.astype(_BF16)
    if routed:
        logits = _dot_f32(wr_ref[...], f_scr[...], _dot_nt)
        eidx = lax.broadcasted_iota(jnp.int32, logits.shape, 0).astype(_F32)
        none = float(N_EXPERTS)
        m1 = jnp.max(logits, axis=0, keepdims=True)
        i1 = jnp.min(jnp.where(logits == m1, eidx, none), axis=0, keepdims=True)
        rest = jnp.where(eidx == i1, -jnp.inf, logits)
        m2 = jnp.max(rest, axis=0, keepdims=True)
        i2 = jnp.min(jnp.where(rest == m2, eidx, none), axis=0, keepdims=True)
        e2 = jnp.exp(m2 - m1)
        w1 = 1.0 / (1.0 + e2)
        w2 = e2 / (1.0 + e2)
        r = lax.broadcasted_iota(jnp.int32, logits.shape, 0)
        route_ref[...] = jnp.where(r == 0, i1, jnp.where(r == 1, i2, jnp.where(r == 2, w1, jnp.where(r == 3, w2, 0.0))))


def _outproj_call(layer, ya, o_f, o_b, gate, x, modtab, g_norm, g_post, g_ffn, w_out, wr_t, route_layer):
    n, d = _stream_shape(x)
    tm = TOKEN_TILE
    routed = wr_t is not None
    tok = lambda w: pl.BlockSpec((tm, w), lambda i: (i, 0))
    a_w = A_HEADS * A_HEAD_DIM
    v_w = B_HEADS * B_DV
    stream_specs = _stream_specs(x, tm)
    in_specs = stream_specs + [
                tok(a_w), tok(v_w), tok(v_w), tok(v_w),
                _layer_block(modtab, layer, (tm // MOD_BLOCK, 8, d), lambda i: (i, 0, 0)),
                _layer_block(g_norm, layer), _layer_block(g_post, layer), _layer_block(g_ffn, layer),
                _resident(w_out, layer)]
    args = _stream_args(x) + [ya, o_f, o_b, gate, modtab, g_norm, g_post, g_ffn, w_out]
    scratch = [pltpu.VMEM((tm, a_w + v_w), _BF16)]
    out_specs = [tok(d), tok(d // 2 if routed else d)]
    out_shape = [jax.ShapeDtypeStruct((n, d), _F32),
                 jax.ShapeDtypeStruct((n, d // 2), jnp.uint32) if routed else jax.ShapeDtypeStruct((n, d), _BF16)]
    if routed:
        scratch.append(pltpu.VMEM((tm, d), _F32))
        in_specs.append(_layer_block(wr_t, route_layer))
        args.append(wr_t)
        out_specs.append(pl.BlockSpec((ROUTE_ROWS, tm), lambda i: (0, i)))
        out_shape.append(jax.ShapeDtypeStruct((ROUTE_ROWS, n), _F32))
    return pl.pallas_call(
        functools.partial(_outproj_kernel, routed=routed, n_stream=len(stream_specs)),
        grid=(n // tm,),
        in_specs=in_specs,
        out_specs=out_specs,
        out_shape=out_shape,
        scratch_shapes=scratch,
        compiler_params=_params(("parallel",)),
        name="out_proj_routed" if routed else "out_proj",
    )(*args)


def _ffn_kernel(f_ref, wg_ref, wu_ref, wd_ref, x_ref, mt_ref, g_ref, o_ref):
    j = pl.program_id(1)

    @pl.when(j == 0)
    def _():
        o_ref[...] = jnp.zeros_like(o_ref)

    f = f_ref[...]
    act = (_silu(_dot(f, wg_ref[...].astype(_BF16))) * _dot(f, wu_ref[...].astype(_BF16))).astype(_BF16)
    o_ref[...] += _dot(act, wd_ref[...].astype(_BF16))

    @pl.when(j == pl.num_programs(1) - 1)
    def _():
        for s in range(o_ref.shape[0] // MOD_BLOCK):
            rows = slice(s * MOD_BLOCK, (s + 1) * MOD_BLOCK)
            o_ref[rows, :] = x_ref[rows, :] + mt_ref[s][5:6] * _rms(o_ref[rows, :], g_ref[...])


def _token_tile(n, target):
    tm = target - target % MOD_BLOCK
    while n % tm:
        tm -= MOD_BLOCK
    return tm


def _ffn_call(layer, ffn_layer, f, w_gate, w_up, w_down, x, modtab, g_post):
    n, d = x.shape
    d_ff = w_gate.shape[2]
    tm = _token_tile(n, FFN_TOKEN_TILE)
    tf = FF_TILE
    return pl.pallas_call(
        _ffn_kernel,
        grid=(n // tm, d_ff // tf),
        in_specs=[
            pl.BlockSpec((tm, d), lambda i, j: (i, 0)),
            _layer_block(w_gate, ffn_layer, (d, tf), lambda i, j: (0, j)),
            _layer_block(w_up, ffn_layer, (d, tf), lambda i, j: (0, j)),
            _layer_block(w_down, ffn_layer, (tf, d), lambda i, j: (j, 0)),
            pl.BlockSpec((tm, d), lambda i, j: (i, 0)),
            _layer_block(modtab, layer, (tm // MOD_BLOCK, 8, d), lambda i, j: (i, 0, 0)),
            _layer_block(g_post, layer),
        ],
        out_specs=pl.BlockSpec((tm, d), lambda i, j: (i, 0)),
        out_shape=jax.ShapeDtypeStruct((n, d), _F32),
        compiler_params=_params(("parallel", "arbitrary")),
        name="dense_ffn",
    )(f, w_gate, w_up, w_down, x, modtab, g_post)


def _rank_kernel(route_ref, rank_ref, count_ref, carry):
    @pl.when(pl.program_id(0) == 0)
    def _():
        carry[...] = jnp.zeros_like(carry)

    tl = route_ref.shape[1]
    r = route_ref[...]
    eidx = lax.broadcasted_iota(jnp.int32, r.shape, 0).astype(_F32)
    oh1 = jnp.where(eidx == r[0:1], 1.0, 0.0)
    oh2 = jnp.where(eidx == r[1:2], 1.0, 0.0)
    both = oh1 + oh2
    before = lax.broadcasted_iota(jnp.int32, (tl, tl), 0) < lax.broadcasted_iota(jnp.int32, (tl, tl), 1)
    seen = carry[:, 0:1] + _dot(both.astype(_BF16), jnp.where(before, 1.0, 0.0).astype(_BF16))
    rank1 = jnp.sum(oh1 * seen, axis=0, keepdims=True)
    rank2 = jnp.sum(oh2 * seen, axis=0, keepdims=True)
    row = lax.broadcasted_iota(jnp.int32, r.shape, 0)
    rank_ref[...] = jnp.where(row == 0, rank1, jnp.where(row == 1, rank2, 0.0))
    carry[...] = carry[...] + jnp.sum(both, axis=1, keepdims=True)
    count_ref[...] = carry[...]


def _rank_call(route):
    n = route.shape[1]
    tl = TOKEN_TILE
    return pl.pallas_call(
        _rank_kernel,
        grid=(n // tl,),
        in_specs=[pl.BlockSpec((ROUTE_ROWS, tl), lambda i: (0, i))],
        out_specs=[pl.BlockSpec((ROUTE_ROWS, tl), lambda i: (0, i)),
                   pl.BlockSpec((ROUTE_ROWS, 128), lambda i: (0, 0))],
        out_shape=[jax.ShapeDtypeStruct((ROUTE_ROWS, n), _F32),
                   jax.ShapeDtypeStruct((ROUTE_ROWS, 128), _F32)],
        scratch_shapes=[pltpu.VMEM((ROUTE_ROWS, 128), _F32)],
        compiler_params=_params(("arbitrary",)),
        name="moe_rank",
    )(route)


def _dispatch_kernel(dest_ref, f_ref, xs_in_ref, xs_ref, sem):
    del xs_in_ref
    td = f_ref.shape[0]

    def row_copy(t, k):
        return pltpu.make_async_copy(f_ref.at[pl.ds(t, 1)], xs_ref.at[pl.ds(dest_ref[0, k * td + t], 1)], sem)

    def issue(t, carry):
        row_copy(t, 0).start()
        row_copy(t, 1).start()
        return carry

    lax.fori_loop(0, td, issue, 0)
    for _ in range(2):
        pltpu.make_async_copy(f_ref, xs_ref.at[pl.ds(0, td)], sem).wait()


def _dispatch_call(dest_blocks, f, xs_init):
    n_steps = dest_blocks.shape[0]
    return pl.pallas_call(
        _dispatch_kernel,
        grid=(n_steps,),
        in_specs=[
            pl.BlockSpec((None, 1, dest_blocks.shape[2]), lambda i: (i, 0, 0), memory_space=pltpu.SMEM),
            pl.BlockSpec((dest_blocks.shape[2] // 2, f.shape[1]), lambda i: (i, 0)),
            pl.BlockSpec(memory_space=pl.ANY),
        ],
        out_specs=pl.BlockSpec(memory_space=pl.ANY),
        out_shape=jax.ShapeDtypeStruct(xs_init.shape, xs_init.dtype),
        scratch_shapes=[pltpu.SemaphoreType.DMA(())],
        input_output_aliases={2: 0},
        compiler_params=_params(("arbitrary",)),
        name="moe_dispatch",
    )(dest_blocks, f, xs_init)


def _expert_kernel(be_ref, rows_ref, xs_ref, wg_ref, wu_ref, wd_ref, ys_ref, xb_scr, wg_scr, wu_scr, wd_scr):
    del be_ref
    n_rows = rows_ref[pl.program_id(0)]

    @pl.when(pl.program_id(1) == 0)
    def _():
        ys_ref[...] = jnp.zeros_like(ys_ref)
        half = xs_ref.shape[1]
        xb_scr[:, :half], xb_scr[:, half:] = _unpack_bf16_pair(xs_ref[...])

    @pl.when(n_rows > 0)
    def _():
        wg_scr[...] = wg_ref[...].astype(_BF16)
        wu_scr[...] = wu_ref[...].astype(_BF16)
        wd_scr[...] = wd_ref[...].astype(_BF16)

    for s in range(EXPERT_ROWS // EXPERT_SUB_ROWS):
        @pl.when(n_rows > s * EXPERT_SUB_ROWS)
        def _():
            rows = slice(s * EXPERT_SUB_ROWS, (s + 1) * EXPERT_SUB_ROWS)
            xb = xb_scr[rows, :]
            act = (_silu(_dot(xb, wg_scr[...])) * _dot(xb, wu_scr[...])).astype(_BF16)
            ys_ref[rows, :] += _dot(act, wd_scr[...])


def _expert_call(moe_layer, block_expert, block_rows, xs, w_gate, w_up, w_down):
    cap = xs.shape[0]
    d, d_ff = w_gate.shape[2:]
    tb = EXPERT_ROWS
    tf = FF_TILE
    n_ff = d_ff // tf

    def ff_of(i, j, rows):
        return jnp.where(rows[i] > 0, j, n_ff - 1)

    grid_spec = pltpu.PrefetchScalarGridSpec(
        num_scalar_prefetch=2,
        grid=(cap // tb, n_ff),
        in_specs=[
            pl.BlockSpec((tb, xs.shape[1]), lambda i, j, be, rows: (i, 0)),
            _layer_block(w_gate, moe_layer, (None, d, tf), lambda i, j, be, rows: (be[i], 0, ff_of(i, j, rows))),
            _layer_block(w_up, moe_layer, (None, d, tf), lambda i, j, be, rows: (be[i], 0, ff_of(i, j, rows))),
            _layer_block(w_down, moe_layer, (None, tf, d), lambda i, j, be, rows: (be[i], ff_of(i, j, rows), 0)),
        ],
        out_specs=pl.BlockSpec((tb, d), lambda i, j, be, rows: (i, 0)),
        scratch_shapes=[pltpu.VMEM((tb, d), _BF16), pltpu.VMEM((d, tf), _BF16), pltpu.VMEM((d, tf), _BF16),
                        pltpu.VMEM((tf, d), _BF16)],
    )
    return pl.pallas_call(
        _expert_kernel,
        grid_spec=grid_spec,
        out_shape=jax.ShapeDtypeStruct((cap, d), _F32),
        compiler_params=_params(("parallel", "arbitrary")),
        name="moe_experts",
    )(block_expert, block_rows, xs, w_gate, w_up, w_down)


def _combine_kernel(dest_ref, ys_ref, w_ref, x_ref, mt_ref, g_ref, o_ref, buf, sem, *, skip_blocks):
    tc = x_ref.shape[0]

    def row_copy(t, k):
        return pltpu.make_async_copy(ys_ref.at[pl.ds(dest_ref[0, k * tc + t], 1)], buf.at[k, pl.ds(t, 1)], sem)

    def issue(t, carry):
        row_copy(t, 0).start()
        row_copy(t, 1).start()
        return carry

    @pl.when(pl.program_id(0) >= skip_blocks)
    def _():
        lax.fori_loop(0, tc, issue, 0)
        for k in range(2):
            pltpu.make_async_copy(ys_ref.at[pl.ds(0, tc)], buf.at[k], sem).wait()
        w = w_ref[...]
        f = w[:, 0:1] * buf[0] + w[:, 1:2] * buf[1]
        for s in range(tc // MOD_BLOCK):
            rows = slice(s * MOD_BLOCK, (s + 1) * MOD_BLOCK)
            o_ref[rows, :] = x_ref[rows, :] + mt_ref[s][5:6] * _rms(f[rows, :], g_ref[...])


def _combine_call(layer, dest_blocks, ys, w_cols, x, modtab, g_post, skip_rows):
    n, d = x.shape
    tc = COMBINE_ROWS
    skip_blocks = skip_rows // tc
    return pl.pallas_call(
        functools.partial(_combine_kernel, skip_blocks=skip_blocks),
        grid=(n // tc,),
        in_specs=[
            pl.BlockSpec((None, 1, 2 * tc), lambda i: (i, 0, 0), memory_space=pltpu.SMEM),
            pl.BlockSpec(memory_space=pl.ANY),
            pl.BlockSpec((tc, 2), lambda i: (i, 0)),
            pl.BlockSpec((tc, d), lambda i: (i, 0)),
            _layer_block(modtab, layer, (tc // MOD_BLOCK, 8, d), lambda i: (i, 0, 0)),
            _layer_block(g_post, layer),
        ],
        out_specs=pl.BlockSpec((tc, d), lambda i: (jnp.maximum(i - skip_blocks, 0), 0)),
        out_shape=jax.ShapeDtypeStruct((n - skip_rows, d), _F32),
        scratch_shapes=[pltpu.VMEM((2, tc, d), _F32), pltpu.SemaphoreType.DMA(())],
        compiler_params=_params(("arbitrary",)),
        name="moe_combine",
    )(dest_blocks, ys, w_cols, x, modtab, g_post)


def _dest_blocks(dest, rows):
    n = dest.shape[1]
    return dest.reshape(2, n // rows, rows).transpose(1, 0, 2).reshape(n // rows, 1, 2 * rows)


def _moe_call(layer, moe_layer, f, route, x, modtab, g_post, w_gate, w_up, w_down, skip_rows):
    n, d = x.shape
    tb = EXPERT_ROWS
    rank, counts = _rank_call(route)
    expert = route[0:2].astype(jnp.int32)
    counts = counts[:, 0].astype(jnp.int32)
    padded = (counts + tb - 1) // tb * tb
    pad_end = jnp.cumsum(padded)
    pad_start = pad_end - padded
    dest = rank[0:2].astype(jnp.int32) + sum(jnp.where(expert == e, pad_start[e], 0) for e in range(N_EXPERTS))
    n_blocks = -(-(2 * n + N_EXPERTS * (tb - 1)) // tb)
    block_start = jnp.arange(n_blocks, dtype=jnp.int32) * tb
    block_expert = jnp.minimum(jnp.sum(block_start[:, None] >= pad_end[None, :], axis=1), N_EXPERTS - 1).astype(jnp.int32)
    sub = EXPERT_SUB_ROWS
    used_end = pad_start + (counts + sub - 1) // sub * sub
    block_used_end = sum(jnp.where(block_expert == e, used_end[e], 0) for e in range(N_EXPERTS))
    block_rows = jnp.clip(block_used_end - block_start, 0, tb).astype(jnp.int32)
    xs = _dispatch_call(_dest_blocks(dest, TOKEN_TILE), f, jnp.zeros((n_blocks * tb, f.shape[1]), f.dtype))
    ys = _expert_call(moe_layer, block_expert, block_rows, xs, w_gate, w_up, w_down)
    return _combine_call(layer, _dest_blocks(dest, COMBINE_ROWS), ys, route[2:4].T, x, modtab, g_post, skip_rows)


def kernel(x, c, ctx, c_ctx, w_mod, b_mod, g_pre_mix, g_post_mix, g_pre_ffn, g_post_ffn, w_in, w_out, g_vnorm, w_spatial, b_spatial, w_decay, b_decay, g_gla_norm, w_ffn_gate, w_ffn_up, w_ffn_down, w_router, w_exp_gate, w_exp_up, w_exp_down):
    n_batch, seq, d = x.shape
    ctx_len = ctx.shape[1]
    depth = w_mod.shape[0]
    a_w = A_HEADS * A_HEAD_DIM
    k_w = B_HEADS * B_DK
    v_w = B_HEADS * B_DV
    main_w = 2 * a_w + 2 * k_w + 2 * v_w
    assert w_in.shape[2] == main_w + 2 * DECAY_RANK and seq % TOKEN_TILE == 0
    assert (n_batch * ctx_len) % TOKEN_TILE == 0 and ctx_len % MOD_BLOCK == 0

    stream = (ctx.reshape(n_batch * ctx_len, d), x.reshape(n_batch * seq, d))

    cvec = jnp.zeros((ROUTE_ROWS, d), _F32).at[:n_batch].set(c).at[n_batch].set(c_ctx)
    mod = _mod_call(cvec, w_mod, b_mod).reshape(depth, ROUTE_ROWS, 6, d)
    per_block = lambda r, blocks: jnp.broadcast_to(mod[:, r:r + 1], (depth, blocks, 6, d))
    modtab = jnp.concatenate([per_block(n_batch, n_batch * ctx_len // MOD_BLOCK)]
                             + [per_block(b, seq // MOD_BLOCK) for b in range(n_batch)], axis=1)
    modtab = jnp.pad(modtab, ((0, 0), (0, 0), (0, 2), (0, 0)))

    w_main = w_in.astype(_BF16)
    w_a = jnp.pad(w_in[:, :, main_w:].astype(_BF16), ((0, 0), (0, 0), (0, CHUNK - 2 * DECAY_RANK)))
    wd = jnp.stack([jnp.pad(w_decay[:, s], ((0, 0), (s * DECAY_RANK, CHUNK - (s + 1) * DECAY_RANK), (0, 0)))
                    for s in range(2)], axis=1)
    bd = b_decay[:, :, None, :]
    w_out_b = w_out.astype(_BF16)
    w_s = w_spatial.astype(_BF16)
    b_s = b_spatial[:, :, :, None]
    wr_t = w_router.transpose(0, 2, 1)
    rows = lambda g: g[:, None, :]
    g_pre_mix, g_post_mix, g_pre_ffn, g_post_ffn, g_vnorm, g_gla_norm = map(
        rows, (g_pre_mix, g_post_mix, g_pre_ffn, g_post_ffn, g_vnorm, g_gla_norm))

    n_ctx = n_batch * ctx_len
    for l in range(depth):
        last = l == depth - 1
        uv, gate, v, qf, kf, qb, kb, cols = _inproj_call(l, stream, modtab, g_pre_mix, w_main, w_a, wd, bd)
        ya = _gmlp_call(l, uv, g_vnorm, w_s, b_s)
        o_f, o_b = _gla_call(qf, kf, qb, kb, v, cols, n_batch, ctx_len // CHUNK, seq // CHUNK)
        routed = l % 2 == 1
        outs = _outproj_call(l, ya, o_f, o_b, gate, stream, modtab, g_gla_norm, g_post_mix, g_pre_ffn, w_out_b,
                             wr_t if routed else None, l // 2)
        if routed:
            stream, f, route = outs
            stream = _moe_call(l, l // 2, f, route, stream, modtab, g_post_ffn, w_exp_gate, w_exp_up, w_exp_down,
                               n_ctx if last else 0)
        else:
            stream, f = outs
            stream = _ffn_call(l, l // 2, f, w_ffn_gate, w_ffn_up, w_ffn_down, stream, modtab, g_post_ffn)
            if last:
                stream = stream[n_ctx:]
    return stream.reshape(n_batch, seq, d)
```

```python
import functools

import jax
import jax.numpy as jnp
from jax import lax
from jax.experimental import pallas as pl
from jax.experimental.pallas import tpu as pltpu

_F32 = jnp.float32
_BF16 = jnp.bfloat16

EPS = 1e-6
CHUNK = 128
GLA_GROUP = 2
MOD_BLOCK = 256
A_HEADS = 8
A_HEAD_DIM = 128
B_HEADS = 4
B_DK = 128
B_DV = 256
DECAY_RANK = 16
DECAY_TAU = 16.0
N_EXPERTS = 8
ROUTE_ROWS = 8
TOKEN_TILE = 512
FFN_TOKEN_TILE = 768
FF_TILE = 256
EXPERT_ROWS = 1024
EXPERT_HALF_ROWS = 512
EXPERT_SUB_ROWS = 256
COMBINE_ROWS = 256
DMA_ISSUE_UNROLL = 8
VMEM_LIMIT = 56 << 20


def _dot(a, b):
    return jnp.dot(a, b, preferred_element_type=_F32)


def _dot_nt(a, b):
    return lax.dot_general(a, b, (((1,), (1,)), ((), ())), preferred_element_type=_F32)


def _dot_tn(a, b):
    return lax.dot_general(a, b, (((0,), (0,)), ((), ())), preferred_element_type=_F32)


def _split(x):
    hi = x.astype(_BF16)
    lo = (x - hi.astype(_F32)).astype(_BF16)
    return hi, lo


def _dot_f32(a, b, dot=_dot):
    ah, al = _split(a)
    bh, bl = _split(b)
    return dot(ah, bh) + dot(ah, bl) + dot(al, bh)


def _rms(x, g):
    return x * lax.rsqrt(jnp.mean(x * x, axis=-1, keepdims=True) + EPS) * g


def _gelu(x):
    return 0.5 * x * (1.0 + lax.erf(x * (2.0 ** -0.5)))


def _silu(x):
    return x * jax.nn.sigmoid(x)


def _log_sigmoid(z):
    return jnp.minimum(z, 0.0) - jnp.log1p(jnp.exp(-jnp.abs(z)))


def _pack_bf16_pair(left, right):
    hi = lax.bitcast_convert_type(left.astype(_BF16).astype(_F32), jnp.uint32)
    lo = lax.bitcast_convert_type(right.astype(_BF16).astype(_F32), jnp.uint32)
    return hi | (lo >> 16)


def _unpack_bf16_pair(packed):
    left = lax.bitcast_convert_type(packed & jnp.uint32(0xFFFF0000), _F32).astype(_BF16)
    right = lax.bitcast_convert_type(packed << 16, _F32).astype(_BF16)
    return left, right


def _params(semantics):
    return pltpu.CompilerParams(dimension_semantics=semantics, vmem_limit_bytes=VMEM_LIMIT)


def _layer_block(arr, layer, block=None, index=None, **kwargs):
    per_layer = arr.shape[1:]
    block = per_layer if block is None else block
    index = (lambda *_: (0,) * len(per_layer)) if index is None else index
    return pl.BlockSpec((None,) + tuple(block), lambda *a: (layer,) + tuple(index(*a)), **kwargs)


def _stream_specs(stream, tm):
    if not isinstance(stream, tuple):
        return [pl.BlockSpec((tm, stream.shape[1]), lambda i: (i, 0))]
    ctx, lat = stream
    assert ctx.shape[0] == tm and lat.shape[0] % tm == 0
    return [pl.BlockSpec((tm, ctx.shape[1]), lambda i: (0, 0)),
            pl.BlockSpec((tm, lat.shape[1]), lambda i: (jnp.maximum(i - 1, 0), 0))]


def _stream_rows(x_refs, rows):
    if len(x_refs) == 1:
        return x_refs[0][rows, :]
    ctx_ref, lat_ref = x_refs
    return jnp.where(pl.program_id(0) == 0, ctx_ref[rows, :], lat_ref[rows, :])


def _stream_shape(stream):
    if not isinstance(stream, tuple):
        return stream.shape
    return stream[0].shape[0] + stream[1].shape[0], stream[0].shape[1]


def _stream_args(stream):
    return list(stream) if isinstance(stream, tuple) else [stream]


def _resident(arr, layer):
    return _layer_block(arr, layer, pipeline_mode=pl.Buffered(1))


def _mod_kernel(c_ref, w_ref, b_ref, o_ref):
    s = _silu(c_ref[...]).astype(_BF16)
    o_ref[0] = _dot(s, w_ref[0].astype(_BF16)) + b_ref[0]


def _mod_call(cvec, w_mod, b_mod):
    depth, d, n = w_mod.shape
    tn = 1024
    return pl.pallas_call(
        _mod_kernel,
        grid=(depth, n // tn),
        in_specs=[
            pl.BlockSpec((ROUTE_ROWS, d), lambda l, j: (0, 0)),
            pl.BlockSpec((1, d, tn), lambda l, j: (l, 0, j)),
            pl.BlockSpec((1, 1, tn), lambda l, j: (l, 0, j)),
        ],
        out_specs=pl.BlockSpec((1, ROUTE_ROWS, tn), lambda l, j: (l, 0, j)),
        out_shape=jax.ShapeDtypeStruct((depth, ROUTE_ROWS, n), _F32),
        compiler_params=_params(("parallel", "parallel")),
        name="adaln_mod",
    )(cvec, w_mod, b_mod.reshape(depth, 1, n))


def _scan_masks(direction):
    row = lax.broadcasted_iota(jnp.int32, (CHUNK, CHUNK), 0)
    col = lax.broadcasted_iota(jnp.int32, (CHUNK, CHUNK), 1)
    return ((col <= row), CHUNK - 1) if direction == 0 else ((col >= row), 0)


def _inproj_kernel(*refs, n_stream):
    x_refs = refs[:n_stream]
    (mt_ref, g_ref, w_ref, wa_ref, wd_ref, bd_ref,
     uv_ref, gate_ref, v_ref, qf_ref, kf_ref, qb_ref, kb_ref, cols_ref, h_scr) = refs[n_stream:]
    tm = x_refs[0].shape[0]
    a_w = A_HEADS * A_HEAD_DIM
    k_w = B_HEADS * B_DK
    v_w = B_HEADS * B_DV
    for s in range(tm // MOD_BLOCK):
        rows = slice(s * MOD_BLOCK, (s + 1) * MOD_BLOCK)
        mt = mt_ref[s]
        h = _rms(_stream_rows(x_refs, rows), g_ref[...]) * (1.0 + mt[1:2]) + mt[0:1]
        h_scr[rows, :] = h.astype(_BF16)
    h = h_scr[...]
    off_q = 2 * a_w
    off_gate = off_q + k_w
    off_k = off_gate + v_w
    off_v = off_k + k_w
    q = _dot(h, w_ref[:, off_q:off_q + k_w]) * (B_DK ** -0.5)
    k = _dot(h, w_ref[:, off_k:off_k + k_w])
    a = _dot(h, wa_ref[...])
    sub = lax.broadcasted_iota(jnp.int32, (CHUNK, k_w), 0)
    factor_rows = [jnp.zeros((CHUNK, k_w), _F32) for _ in range(tm // CHUNK)]
    for d, (q_ref, k_ref) in enumerate(((qf_ref, kf_ref), (qb_ref, kb_ref))):
        scanned, end = _scan_masks(d)
        tri = jnp.where(scanned, 1.0, 0.0).astype(_BF16)
        la = _log_sigmoid(_dot_f32(a, wd_ref[d]) + bd_ref[d]) * (1.0 / DECAY_TAU)
        la_hi, la_lo = _split(la)
        for c in range(tm // CHUNK):
            rows = slice(c * CHUNK, (c + 1) * CHUNK)
            b = _dot(tri, la_hi[rows]) + _dot(tri, la_lo[rows])
            b_mid = b[CHUNK // 2:CHUNK // 2 + 1]
            b_end = b[end:end + 1]
            q_ref[rows, :] = (q[rows] * jnp.exp(b - b_mid)).astype(_BF16)
            kd = k[rows] * jnp.exp(b_mid - b)
            for hd in range(B_HEADS):
                kc = slice(hd * B_DK, (hd + 1) * B_DK)
                k_ref[c, kc, :] = kd[:, kc].T.astype(_BF16)
            for r, factor in enumerate((jnp.exp(b_mid), jnp.exp(b_end - b_mid), jnp.exp(b_end))):
                factor_rows[c] = jnp.where(sub == 3 * d + r, factor, factor_rows[c])
    for c in range(tm // CHUNK):
        for hd in range(B_HEADS):
            kc = slice(hd * B_DK, (hd + 1) * B_DK)
            cols_ref[c, kc, :] = factor_rows[c][:, kc].T
    for c0 in range(0, 2 * a_w, a_w):
        uv_ref[:, c0:c0 + a_w] = _dot(h, w_ref[:, c0:c0 + a_w]).astype(_BF16)
    gate_ref[...] = _dot(h, w_ref[:, off_gate:off_gate + v_w]).astype(_BF16)
    v_ref[...] = _dot(h, w_ref[:, off_v:off_v + v_w]).astype(_BF16)


def _inproj_call(layer, x, modtab, g_pre, w_main, w_a, wd, bd):
    n, d = _stream_shape(x)
    tm = TOKEN_TILE
    a_w = A_HEADS * A_HEAD_DIM
    k_w = B_HEADS * B_DK
    v_w = B_HEADS * B_DV
    tok = lambda w: pl.BlockSpec((tm, w), lambda i: (i, 0))
    tok_out = lambda w: (tok(w), jax.ShapeDtypeStruct((n, w), _BF16))
    per_chunk = lambda dtype: (pl.BlockSpec((tm // CHUNK, k_w, CHUNK), lambda i: (i, 0, 0)),
                               jax.ShapeDtypeStruct((n // CHUNK, k_w, CHUNK), dtype))
    outs = [tok_out(2 * a_w), tok_out(v_w), tok_out(v_w), tok_out(k_w), per_chunk(_BF16), tok_out(k_w),
            per_chunk(_BF16), per_chunk(_F32)]
    stream_specs = _stream_specs(x, tm)
    return pl.pallas_call(
        functools.partial(_inproj_kernel, n_stream=len(stream_specs)),
        grid=(n // tm,),
        in_specs=stream_specs + [
            _layer_block(modtab, layer, (tm // MOD_BLOCK, 8, d), lambda i: (i, 0, 0)),
            _layer_block(g_pre, layer),
            _resident(w_main, layer),
            _resident(w_a, layer),
            _layer_block(wd, layer),
            _layer_block(bd, layer),
        ],
        out_specs=[spec for spec, _ in outs],
        out_shape=[shape for _, shape in outs],
        scratch_shapes=[pltpu.VMEM((tm, d), _BF16)],
        compiler_params=_params(("parallel",)),
        name="in_proj",
    )(*_stream_args(x), modtab, g_pre, w_main, w_a, wd, bd)


def _gmlp_kernel(uv_ref, gv_ref, ws_ref, bs_ref, y_ref):
    tm = uv_ref.shape[0]
    a_w = A_HEADS * A_HEAD_DIM
    for c in range(tm // CHUNK):
        rows = slice(c * CHUNK, (c + 1) * CHUNK)
        u = _gelu(uv_ref[rows, :a_w].astype(_F32))
        v = _rms(_gelu(uv_ref[rows, a_w:].astype(_F32)), gv_ref[...]).astype(_BF16)
        for h in range(A_HEADS):
            cols = slice(h * A_HEAD_DIM, (h + 1) * A_HEAD_DIM)
            s = _dot(ws_ref[h], v[:, cols]) + bs_ref[h]
            y_ref[rows, cols] = (u[:, cols] * s).astype(_BF16)


def _gmlp_call(layer, uv, g_v, w_s, b_s):
    n = uv.shape[0]
    tm = TOKEN_TILE
    a_w = A_HEADS * A_HEAD_DIM
    return pl.pallas_call(
        _gmlp_kernel,
        grid=(n // tm,),
        in_specs=[
            pl.BlockSpec((tm, 2 * a_w), lambda i: (i, 0)),
            _layer_block(g_v, layer),
            _layer_block(w_s, layer),
            _layer_block(b_s, layer),
        ],
        out_specs=pl.BlockSpec((tm, a_w), lambda i: (i, 0)),
        out_shape=jax.ShapeDtypeStruct((n, a_w), _BF16),
        compiler_params=_params(("parallel",)),
        name="gmlp",
    )(uv, g_v, w_s, b_s)


def _gla_kernel(qf_ref, kf_ref, vf_ref, cf_ref, qb_ref, kb_ref, vb_ref, cb_ref, of_ref, ob_ref, s_ref):
    @pl.when(pl.program_id(1) == 0)
    def _():
        s_ref[...] = jnp.zeros_like(s_ref)

    group = qf_ref.shape[0] // CHUNK
    streams = ((qf_ref, kf_ref, vf_ref, cf_ref, of_ref), (qb_ref, kb_ref, vb_ref, cb_ref, ob_ref))
    for g in range(group):
        for d, (q_ref, k_ref, v_ref, c_ref, o_ref) in enumerate(streams):
            scanned, _ = _scan_masks(d)
            c = g if d == 0 else group - 1 - g
            rows = slice(c * CHUNK, (c + 1) * CHUNK)
            for h in range(B_HEADS):
                kc = slice(h * B_DK, (h + 1) * B_DK)
                vc = slice(h * B_DV, (h + 1) * B_DV)
                cols = c_ref[c, kc, :]
                e_mid, e_end, decay = (cols[:, 3 * d + r:3 * d + r + 1] for r in range(3))
                qd = q_ref[rows, kc]
                kd_t = k_ref[c, kc, :]
                vh = v_ref[rows, vc]
                state = s_ref[d, h]
                scores = jnp.where(scanned, _dot(qd, kd_t), 0.0).astype(_BF16)
                o_ref[rows, vc] = _dot(scores, vh) + _dot(qd, (e_mid * state).astype(_BF16))
                s_ref[d, h] = decay * state + e_end * _dot(kd_t, vh)


def _gla_call(qf, kf, qb, kb, v, cols, n_batch, ctx_chunks, lat_chunks):
    n = v.shape[0]
    k_w = B_HEADS * B_DK
    v_w = B_HEADS * B_DV
    group = GLA_GROUP
    assert ctx_chunks % group == 0 and lat_chunks % group == 0
    ctx_steps = ctx_chunks // group
    lat_steps = lat_chunks // group
    steps = ctx_steps + lat_steps
    lat0 = n_batch * ctx_steps

    def fwd(b, t):
        return jnp.where(t < ctx_steps, b * ctx_steps + t, lat0 + b * lat_steps + (t - ctx_steps))

    def bwd(b, t):
        return jnp.where(t < ctx_steps, b * ctx_steps + (ctx_steps - 1 - t),
                         lat0 + b * lat_steps + (steps - 1 - t))

    def specs(group_of):
        tok = lambda w: pl.BlockSpec((group * CHUNK, w), lambda b, t: (group_of(b, t), 0))
        per_chunk = pl.BlockSpec((group, k_w, CHUNK), lambda b, t: (group_of(b, t), 0, 0))
        return [tok(k_w), per_chunk, tok(v_w), per_chunk]

    return pl.pallas_call(
        _gla_kernel,
        grid=(n_batch, steps),
        in_specs=specs(fwd) + specs(bwd),
        out_specs=[pl.BlockSpec((group * CHUNK, v_w), lambda b, t: (fwd(b, t), 0)),
                   pl.BlockSpec((group * CHUNK, v_w), lambda b, t: (bwd(b, t), 0))],
        out_shape=[jax.ShapeDtypeStruct((n, v_w), _F32)] * 2,
        scratch_shapes=[pltpu.VMEM((2, B_HEADS, B_DK, B_DV), _F32)],
        compiler_params=_params(("parallel", "arbitrary")),
        name="gla_scan",
    )(qf, kf, v, cols, qb, kb, v, cols)


def _outproj_kernel(*refs, routed, n_stream):
    x_refs, refs = refs[:n_stream], refs[n_stream:]
    if routed:
        (ya_ref, of_ref, ob_ref, gate_ref, mt_ref, gn_ref, gpost_ref, gffn_ref, w_ref, wr_ref,
         xo_ref, f_ref, route_ref, y_scr, f_scr) = refs
    else:
        (ya_ref, of_ref, ob_ref, gate_ref, mt_ref, gn_ref, gpost_ref, gffn_ref, w_ref,
         xo_ref, f_ref, y_scr) = refs
    a_w = A_HEADS * A_HEAD_DIM
    tm = x_refs[0].shape[0]
    for s in range(tm // MOD_BLOCK):
        rows = slice(s * MOD_BLOCK, (s + 1) * MOD_BLOCK)
        y_scr[rows, :a_w] = ya_ref[rows, :]
        for h in range(B_HEADS):
            vc = slice(h * B_DV, (h + 1) * B_DV)
            o = of_ref[rows, vc] + ob_ref[rows, vc]
            o = o * lax.rsqrt(jnp.mean(o * o, axis=-1, keepdims=True) + EPS) * gn_ref[:, vc]
            y_scr[rows, a_w + h * B_DV:a_w + (h + 1) * B_DV] = (
                o * _silu(gate_ref[rows, vc].astype(_F32))).astype(_BF16)
        m = _rms(_dot(y_scr[rows, :], w_ref[...]), gpost_ref[...])
        mt = mt_ref[s]
        x = _stream_rows(x_refs, rows) + mt[2:3] * m
        xo_ref[rows, :] = x
        f = _rms(x, gffn_ref[...]) * (1.0 + mt[4:5]) + mt[3:4]
        if routed:
            f_scr[rows, :] = f
            half = f.shape[1] // 2
            f_ref[rows, :] = _pack_bf16_pair(f[:, :half], f[:, half:])
        else:
            f_ref[rows, :] = f.astype(_BF16)
    if routed:
        logits = _dot_f32(wr_ref[...], f_scr[...], _dot_nt)
        eidx = lax.broadcasted_iota(jnp.int32, logits.shape, 0).astype(_F32)
        none = float(N_EXPERTS)
        m1 = jnp.max(logits, axis=0, keepdims=True)
        i1 = jnp.min(jnp.where(logits == m1, eidx, none), axis=0, keepdims=True)
        rest = jnp.where(eidx == i1, -jnp.inf, logits)
        m2 = jnp.max(rest, axis=0, keepdims=True)
        i2 = jnp.min(jnp.where(rest == m2, eidx, none), axis=0, keepdims=True)
        e2 = jnp.exp(m2 - m1)
        w1 = 1.0 / (1.0 + e2)
        w2 = e2 / (1.0 + e2)
        r = lax.broadcasted_iota(jnp.int32, logits.shape, 0)
        route_ref[...] = jnp.where(r == 0, i1, jnp.where(r == 1, i2, jnp.where(r == 2, w1, jnp.where(r == 3, w2, 0.0))))


def _outproj_call(layer, ya, o_f, o_b, gate, x, modtab, g_norm, g_post, g_ffn, w_out, wr_t, route_layer):
    n, d = _stream_shape(x)
    tm = TOKEN_TILE
    routed = wr_t is not None
    tok = lambda w: pl.BlockSpec((tm, w), lambda i: (i, 0))
    a_w = A_HEADS * A_HEAD_DIM
    v_w = B_HEADS * B_DV
    stream_specs = _stream_specs(x, tm)
    in_specs = stream_specs + [
                tok(a_w), tok(v_w), tok(v_w), tok(v_w),
                _layer_block(modtab, layer, (tm // MOD_BLOCK, 8, d), lambda i: (i, 0, 0)),
                _layer_block(g_norm, layer), _layer_block(g_post, layer), _layer_block(g_ffn, layer),
                _resident(w_out, layer)]
    args = _stream_args(x) + [ya, o_f, o_b, gate, modtab, g_norm, g_post, g_ffn, w_out]
    scratch = [pltpu.VMEM((tm, a_w + v_w), _BF16)]
    out_specs = [tok(d), tok(d // 2 if routed else d)]
    out_shape = [jax.ShapeDtypeStruct((n, d), _F32),
                 jax.ShapeDtypeStruct((n, d // 2), jnp.uint32) if routed else jax.ShapeDtypeStruct((n, d), _BF16)]
    if routed:
        scratch.append(pltpu.VMEM((tm, d), _F32))
        in_specs.append(_layer_block(wr_t, route_layer))
        args.append(wr_t)
        out_specs.append(pl.BlockSpec((ROUTE_ROWS, tm), lambda i: (0, i)))
        out_shape.append(jax.ShapeDtypeStruct((ROUTE_ROWS, n), _F32))
    return pl.pallas_call(
        functools.partial(_outproj_kernel, routed=routed, n_stream=len(stream_specs)),
        grid=(n // tm,),
        in_specs=in_specs,
        out_specs=out_specs,
        out_shape=out_shape,
        scratch_shapes=scratch,
        compiler_params=_params(("parallel",)),
        name="out_proj_routed" if routed else "out_proj",
    )(*args)


def _ffn_kernel(f_ref, wg_ref, wu_ref, wd_ref, x_ref, mt_ref, g_ref, o_ref):
    j = pl.program_id(1)

    @pl.when(j == 0)
    def _():
        o_ref[...] = jnp.zeros_like(o_ref)

    f = f_ref[...]
    act = (_silu(_dot(f, wg_ref[...].astype(_BF16))) * _dot(f, wu_ref[...].astype(_BF16))).astype(_BF16)
    o_ref[...] += _dot(act, wd_ref[...].astype(_BF16))

    @pl.when(j == pl.num_programs(1) - 1)
    def _():
        for s in range(o_ref.shape[0] // MOD_BLOCK):
            rows = slice(s * MOD_BLOCK, (s + 1) * MOD_BLOCK)
            o_ref[rows, :] = x_ref[rows, :] + mt_ref[s][5:6] * _rms(o_ref[rows, :], g_ref[...])


def _token_tile(n, target):
    tm = target - target % MOD_BLOCK
    while n % tm:
        tm -= MOD_BLOCK
    return tm


def _ffn_call(layer, ffn_layer, f, w_gate, w_up, w_down, x, modtab, g_post):
    n, d = x.shape
    d_ff = w_gate.shape[2]
    tm = _token_tile(n, FFN_TOKEN_TILE)
    tf = FF_TILE
    return pl.pallas_call(
        _ffn_kernel,
        grid=(n // tm, d_ff // tf),
        in_specs=[
            pl.BlockSpec((tm, d), lambda i, j: (i, 0)),
            _layer_block(w_gate, ffn_layer, (d, tf), lambda i, j: (0, j)),
            _layer_block(w_up, ffn_layer, (d, tf), lambda i, j: (0, j)),
            _layer_block(w_down, ffn_layer, (tf, d), lambda i, j: (j, 0)),
            pl.BlockSpec((tm, d), lambda i, j: (i, 0)),
            _layer_block(modtab, layer, (tm // MOD_BLOCK, 8, d), lambda i, j: (i, 0, 0)),
            _layer_block(g_post, layer),
        ],
        out_specs=pl.BlockSpec((tm, d), lambda i, j: (i, 0)),
        out_shape=jax.ShapeDtypeStruct((n, d), _F32),
        compiler_params=_params(("parallel", "arbitrary")),
        name="dense_ffn",
    )(f, w_gate, w_up, w_down, x, modtab, g_post)


def _rank_kernel(route_ref, rank_ref, count_ref, carry):
    @pl.when(pl.program_id(0) == 0)
    def _():
        carry[...] = jnp.zeros_like(carry)

    tl = route_ref.shape[1]
    r = route_ref[...]
    eidx = lax.broadcasted_iota(jnp.int32, r.shape, 0).astype(_F32)
    oh1 = jnp.where(eidx == r[0:1], 1.0, 0.0)
    oh2 = jnp.where(eidx == r[1:2], 1.0, 0.0)
    both = oh1 + oh2
    before = lax.broadcasted_iota(jnp.int32, (tl, tl), 0) < lax.broadcasted_iota(jnp.int32, (tl, tl), 1)
    seen = carry[:, 0:1] + _dot(both.astype(_BF16), jnp.where(before, 1.0, 0.0).astype(_BF16))
    rank1 = jnp.sum(oh1 * seen, axis=0, keepdims=True)
    rank2 = jnp.sum(oh2 * seen, axis=0, keepdims=True)
    row = lax.broadcasted_iota(jnp.int32, r.shape, 0)
    rank_ref[...] = jnp.where(row == 0, rank1, jnp.where(row == 1, rank2, 0.0))
    carry[...] = carry[...] + jnp.sum(both, axis=1, keepdims=True)
    count_ref[...] = carry[...]


def _rank_call(route):
    n = route.shape[1]
    tl = TOKEN_TILE
    return pl.pallas_call(
        _rank_kernel,
        grid=(n // tl,),
        in_specs=[pl.BlockSpec((ROUTE_ROWS, tl), lambda i: (0, i))],
        out_specs=[pl.BlockSpec((ROUTE_ROWS, tl), lambda i: (0, i)),
                   pl.BlockSpec((ROUTE_ROWS, 128), lambda i: (0, 0))],
        out_shape=[jax.ShapeDtypeStruct((ROUTE_ROWS, n), _F32),
                   jax.ShapeDtypeStruct((ROUTE_ROWS, 128), _F32)],
        scratch_shapes=[pltpu.VMEM((ROUTE_ROWS, 128), _F32)],
        compiler_params=_params(("arbitrary",)),
        name="moe_rank",
    )(route)


def _dispatch_kernel(dest_ref, f_ref, xs_in_ref, xs_ref, sem):
    del xs_in_ref
    td = f_ref.shape[0]

    def row_copy(t, k):
        return pltpu.make_async_copy(f_ref.at[pl.ds(t, 1)], xs_ref.at[pl.ds(dest_ref[0, k * td + t], 1)], sem)

    def issue(t, carry):
        row_copy(t, 0).start()
        row_copy(t, 1).start()
        return carry

    lax.fori_loop(0, td, issue, 0, unroll=DMA_ISSUE_UNROLL)
    for _ in range(2):
        pltpu.make_async_copy(f_ref, xs_ref.at[pl.ds(0, td)], sem).wait()


def _dispatch_call(dest_blocks, f, xs_init):
    n_steps = dest_blocks.shape[0]
    return pl.pallas_call(
        _dispatch_kernel,
        grid=(n_steps,),
        in_specs=[
            pl.BlockSpec((None, 1, dest_blocks.shape[2]), lambda i: (i, 0, 0), memory_space=pltpu.SMEM),
            pl.BlockSpec((dest_blocks.shape[2] // 2, f.shape[1]), lambda i: (i, 0)),
            pl.BlockSpec(memory_space=pl.ANY),
        ],
        out_specs=pl.BlockSpec(memory_space=pl.ANY),
        out_shape=jax.ShapeDtypeStruct(xs_init.shape, xs_init.dtype),
        scratch_shapes=[pltpu.SemaphoreType.DMA(())],
        input_output_aliases={2: 0},
        compiler_params=_params(("arbitrary",)),
        name="moe_dispatch",
    )(dest_blocks, f, xs_init)


def _expert_kernel(be_ref, rows_ref, xs_ref, wg_ref, wu_ref, wd_ref, ys_ref, xb_scr):
    del be_ref
    n_rows = rows_ref[pl.program_id(0)]

    @pl.when(pl.program_id(1) == 0)
    def _():
        ys_ref[...] = jnp.zeros_like(ys_ref)
        half = xs_ref.shape[1]
        xb_scr[:, :half], xb_scr[:, half:] = _unpack_bf16_pair(xs_ref[...])

    def swiglu_pieces(*pieces):
        wg, wu, wd = (ref[...].astype(_BF16) for ref in (wg_ref, wu_ref, wd_ref))
        for rows in pieces:
            xb = xb_scr[rows, :]
            act = (_silu(_dot(xb, wg)) * _dot(xb, wu)).astype(_BF16)
            ys_ref[rows, :] += _dot(act, wd)

    halves = [slice(b, b + EXPERT_HALF_ROWS) for b in range(0, EXPERT_ROWS, EXPERT_HALF_ROWS)]
    pl.when(n_rows == EXPERT_ROWS)(functools.partial(swiglu_pieces, *halves))
    for rows in halves:
        base = rows.start
        pl.when((n_rows >= base + EXPERT_HALF_ROWS) & (n_rows < EXPERT_ROWS))(functools.partial(swiglu_pieces, rows))
        for part in range(EXPERT_SUB_ROWS, EXPERT_HALF_ROWS, EXPERT_SUB_ROWS):
            pl.when(n_rows == base + part)(functools.partial(swiglu_pieces, slice(base, base + part)))


def _expert_call(moe_layer, block_expert, block_rows, xs, w_gate, w_up, w_down):
    cap = xs.shape[0]
    d, d_ff = w_gate.shape[2:]
    tb = EXPERT_ROWS
    tf = FF_TILE
    n_ff = d_ff // tf

    def ff_of(i, j, rows):
        return jnp.where(rows[i] > 0, j, n_ff - 1)

    grid_spec = pltpu.PrefetchScalarGridSpec(
        num_scalar_prefetch=2,
        grid=(cap // tb, n_ff),
        in_specs=[
            pl.BlockSpec((tb, xs.shape[1]), lambda i, j, be, rows: (i, 0)),
            _layer_block(w_gate, moe_layer, (None, d, tf), lambda i, j, be, rows: (be[i], 0, ff_of(i, j, rows))),
            _layer_block(w_up, moe_layer, (None, d, tf), lambda i, j, be, rows: (be[i], 0, ff_of(i, j, rows))),
            _layer_block(w_down, moe_layer, (None, tf, d), lambda i, j, be, rows: (be[i], ff_of(i, j, rows), 0)),
        ],
        out_specs=pl.BlockSpec((tb, d), lambda i, j, be, rows: (i, 0)),
        scratch_shapes=[pltpu.VMEM((tb, d), _BF16)],
    )
    return pl.pallas_call(
        _expert_kernel,
        grid_spec=grid_spec,
        out_shape=jax.ShapeDtypeStruct((cap, d), _F32),
        compiler_params=_params(("parallel", "arbitrary")),
        name="moe_experts",
    )(block_expert, block_rows, xs, w_gate, w_up, w_down)


def _combine_kernel(dest_ref, ys_ref, w_ref, x_ref, mt_ref, g_ref, o_ref, buf, sem, *, skip_blocks):
    tc = x_ref.shape[0]

    def row_copy(t, k):
        return pltpu.make_async_copy(ys_ref.at[pl.ds(dest_ref[0, k * tc + t], 1)], buf.at[k, pl.ds(t, 1)], sem)

    def issue(t, carry):
        row_copy(t, 0).start()
        row_copy(t, 1).start()
        return carry

    @pl.when(pl.program_id(0) >= skip_blocks)
    def _():
        lax.fori_loop(0, tc, issue, 0, unroll=DMA_ISSUE_UNROLL)
        for k in range(2):
            pltpu.make_async_copy(ys_ref.at[pl.ds(0, tc)], buf.at[k], sem).wait()
        w = w_ref[...]
        f = w[:, 0:1] * buf[0] + w[:, 1:2] * buf[1]
        for s in range(tc // MOD_BLOCK):
            rows = slice(s * MOD_BLOCK, (s + 1) * MOD_BLOCK)
            o_ref[rows, :] = x_ref[rows, :] + mt_ref[s][5:6] * _rms(f[rows, :], g_ref[...])


def _combine_call(layer, dest_blocks, ys, w_cols, x, modtab, g_post, skip_rows):
    n, d = x.shape
    tc = COMBINE_ROWS
    skip_blocks = skip_rows // tc
    return pl.pallas_call(
        functools.partial(_combine_kernel, skip_blocks=skip_blocks),
        grid=(n // tc,),
        in_specs=[
            pl.BlockSpec((None, 1, 2 * tc), lambda i: (i, 0, 0), memory_space=pltpu.SMEM),
            pl.BlockSpec(memory_space=pl.ANY),
            pl.BlockSpec((tc, 2), lambda i: (i, 0)),
            pl.BlockSpec((tc, d), lambda i: (i, 0)),
            _layer_block(modtab, layer, (tc // MOD_BLOCK, 8, d), lambda i: (i, 0, 0)),
            _layer_block(g_post, layer),
        ],
        out_specs=pl.BlockSpec((tc, d), lambda i: (jnp.maximum(i - skip_blocks, 0), 0)),
        out_shape=jax.ShapeDtypeStruct((n - skip_rows, d), _F32),
        scratch_shapes=[pltpu.VMEM((2, tc, d), _F32), pltpu.SemaphoreType.DMA(())],
        compiler_params=_params(("arbitrary",)),
        name="moe_combine",
    )(dest_blocks, ys, w_cols, x, modtab, g_post)


def _dest_blocks(dest, rows):
    n = dest.shape[1]
    return dest.reshape(2, n // rows, rows).transpose(1, 0, 2).reshape(n // rows, 1, 2 * rows)


def _moe_call(layer, moe_layer, f, route, x, modtab, g_post, w_gate, w_up, w_down, skip_rows):
    n, d = x.shape
    tb = EXPERT_ROWS
    rank, counts = _rank_call(route)
    expert = route[0:2].astype(jnp.int32)
    counts = counts[:, 0].astype(jnp.int32)
    padded = (counts + tb - 1) // tb * tb
    pad_end = jnp.cumsum(padded)
    pad_start = pad_end - padded
    dest = rank[0:2].astype(jnp.int32) + sum(jnp.where(expert == e, pad_start[e], 0) for e in range(N_EXPERTS))
    n_blocks = -(-(2 * n + N_EXPERTS * (tb - 1)) // tb)
    block_start = jnp.arange(n_blocks, dtype=jnp.int32) * tb
    block_expert = jnp.minimum(jnp.sum(block_start[:, None] >= pad_end[None, :], axis=1), N_EXPERTS - 1).astype(jnp.int32)
    sub = EXPERT_SUB_ROWS
    used_end = pad_start + (counts + sub - 1) // sub * sub
    block_used_end = sum(jnp.where(block_expert == e, used_end[e], 0) for e in range(N_EXPERTS))
    block_rows = jnp.clip(block_used_end - block_start, 0, tb).astype(jnp.int32)
    xs = _dispatch_call(_dest_blocks(dest, TOKEN_TILE), f, jnp.zeros((n_blocks * tb, f.shape[1]), f.dtype))
    ys = _expert_call(moe_layer, block_expert, block_rows, xs, w_gate, w_up, w_down)
    return _combine_call(layer, _dest_blocks(dest, COMBINE_ROWS), ys, route[2:4].T, x, modtab, g_post, skip_rows)


def kernel(x, c, ctx, c_ctx, w_mod, b_mod, g_pre_mix, g_post_mix, g_pre_ffn, g_post_ffn, w_in, w_out, g_vnorm, w_spatial, b_spatial, w_decay, b_decay, g_gla_norm, w_ffn_gate, w_ffn_up, w_ffn_down, w_router, w_exp_gate, w_exp_up, w_exp_down):
    n_batch, seq, d = x.shape
    ctx_len = ctx.shape[1]
    depth = w_mod.shape[0]
    a_w = A_HEADS * A_HEAD_DIM
    k_w = B_HEADS * B_DK
    v_w = B_HEADS * B_DV
    main_w = 2 * a_w + 2 * k_w + 2 * v_w
    assert w_in.shape[2] == main_w + 2 * DECAY_RANK and seq % TOKEN_TILE == 0
    assert (n_batch * ctx_len) % TOKEN_TILE == 0 and ctx_len % MOD_BLOCK == 0

    stream = (ctx.reshape(n_batch * ctx_len, d), x.reshape(n_batch * seq, d))

    cvec = jnp.zeros((ROUTE_ROWS, d), _F32).at[:n_batch].set(c).at[n_batch].set(c_ctx)
    mod = _mod_call(cvec, w_mod, b_mod).reshape(depth, ROUTE_ROWS, 6, d)
    per_block = lambda r, blocks: jnp.broadcast_to(mod[:, r:r + 1], (depth, blocks, 6, d))
    modtab = jnp.concatenate([per_block(n_batch, n_batch * ctx_len // MOD_BLOCK)]
                             + [per_block(b, seq // MOD_BLOCK) for b in range(n_batch)], axis=1)
    modtab = jnp.pad(modtab, ((0, 0), (0, 0), (0, 2), (0, 0)))

    w_main = w_in.astype(_BF16)
    w_a = jnp.pad(w_in[:, :, main_w:].astype(_BF16), ((0, 0), (0, 0), (0, CHUNK - 2 * DECAY_RANK)))
    wd = jnp.stack([jnp.pad(w_decay[:, s], ((0, 0), (s * DECAY_RANK, CHUNK - (s + 1) * DECAY_RANK), (0, 0)))
                    for s in range(2)], axis=1)
    bd = b_decay[:, :, None, :]
    w_out_b = w_out.astype(_BF16)
    w_s = w_spatial.astype(_BF16)
    b_s = b_spatial[:, :, :, None]
    wr_t = w_router.transpose(0, 2, 1)
    rows = lambda g: g[:, None, :]
    g_pre_mix, g_post_mix, g_pre_ffn, g_post_ffn, g_vnorm, g_gla_norm = map(
        rows, (g_pre_mix, g_post_mix, g_pre_ffn, g_post_ffn, g_vnorm, g_gla_norm))

    n_ctx = n_batch * ctx_len
    for l in range(depth):
        last = l == depth - 1
        uv, gate, v, qf, kf, qb, kb, cols = _inproj_call(l, stream, modtab, g_pre_mix, w_main, w_a, wd, bd)
        ya = _gmlp_call(l, uv, g_vnorm, w_s, b_s)
        o_f, o_b = _gla_call(qf, kf, qb, kb, v, cols, n_batch, ctx_len // CHUNK, seq // CHUNK)
        routed = l % 2 == 1
        outs = _outproj_call(l, ya, o_f, o_b, gate, stream, modtab, g_gla_norm, g_post_mix, g_pre_ffn, w_out_b,
                             wr_t if routed else None, l // 2)
        if routed:
            stream, f, route = outs
            stream = _moe_call(l, l // 2, f, route, stream, modtab, g_post_ffn, w_exp_gate, w_exp_up, w_exp_down,
                               n_ctx if last else 0)
        else:
            stream, f = outs
            stream = _ffn_call(l, l // 2, f, w_ffn_gate, w_ffn_up, w_ffn_down, stream, modtab, g_post_ffn)
            if last:
                stream = stream[n_ctx:]
    return stream.reshape(n_batch, seq, d)
```

```python
import functools

import jax
import jax.numpy as jnp
from jax import lax
from jax.experimental import pallas as pl
from jax.experimental.pallas import tpu as pltpu

_F32 = jnp.float32
_BF16 = jnp.bfloat16

EPS = 1e-6
CHUNK = 128
GLA_GROUP = 2
MOD_BLOCK = 256
A_HEADS = 8
A_HEAD_DIM = 128
B_HEADS = 4
B_DK = 128
B_DV = 256
DECAY_RANK = 16
DECAY_TAU = 16.0
N_EXPERTS = 8
ROUTE_ROWS = 8
TOKEN_TILE = 512
FFN_TOKEN_TILE = 768
FF_TILE = 256
EXPERT_ROWS = 1024
EXPERT_HALF_ROWS = 512
EXPERT_SUB_ROWS = 256
COMBINE_ROWS = 256
DMA_ISSUE_UNROLL = 8
VMEM_LIMIT = 56 << 20


def _dot(a, b):
    return jnp.dot(a, b, preferred_element_type=_F32)


def _dot_nt(a, b):
    return lax.dot_general(a, b, (((1,), (1,)), ((), ())), preferred_element_type=_F32)


def _dot_tn(a, b):
    return lax.dot_general(a, b, (((0,), (0,)), ((), ())), preferred_element_type=_F32)


def _split(x):
    hi = x.astype(_BF16)
    lo = (x - hi.astype(_F32)).astype(_BF16)
    return hi, lo


def _dot_f32(a, b, dot=_dot):
    ah, al = _split(a)
    bh, bl = _split(b)
    return dot(ah, bh) + dot(ah, bl) + dot(al, bh)


def _rms(x, g):
    return x * lax.rsqrt(jnp.mean(x * x, axis=-1, keepdims=True) + EPS) * g


def _gelu(x):
    return 0.5 * x * (1.0 + lax.erf(x * (2.0 ** -0.5)))


def _silu(x):
    return x * jax.nn.sigmoid(x)


def _log_sigmoid(z):
    return jnp.minimum(z, 0.0) - jnp.log1p(jnp.exp(-jnp.abs(z)))


def _pack_bf16_pair(left, right):
    hi = lax.bitcast_convert_type(left.astype(_BF16).astype(_F32), jnp.uint32)
    lo = lax.bitcast_convert_type(right.astype(_BF16).astype(_F32), jnp.uint32)
    return hi | (lo >> 16)


def _unpack_bf16_pair(packed):
    left = lax.bitcast_convert_type(packed & jnp.uint32(0xFFFF0000), _F32).astype(_BF16)
    right = lax.bitcast_convert_type(packed << 16, _F32).astype(_BF16)
    return left, right


def _params(semantics):
    return pltpu.CompilerParams(dimension_semantics=semantics, vmem_limit_bytes=VMEM_LIMIT)


def _layer_block(arr, layer, block=None, index=None, **kwargs):
    per_layer = arr.shape[1:]
    block = per_layer if block is None else block
    index = (lambda *_: (0,) * len(per_layer)) if index is None else index
    return pl.BlockSpec((None,) + tuple(block), lambda *a: (layer,) + tuple(index(*a)), **kwargs)


def _stream_specs(stream, tm):
    if not isinstance(stream, tuple):
        return [pl.BlockSpec((tm, stream.shape[1]), lambda i: (i, 0))]
    ctx, lat = stream
    assert ctx.shape[0] == tm and lat.shape[0] % tm == 0
    return [pl.BlockSpec((tm, ctx.shape[1]), lambda i: (0, 0)),
            pl.BlockSpec((tm, lat.shape[1]), lambda i: (jnp.maximum(i - 1, 0), 0))]


def _stream_rows(x_refs, rows):
    if len(x_refs) == 1:
        return x_refs[0][rows, :]
    ctx_ref, lat_ref = x_refs
    return jnp.where(pl.program_id(0) == 0, ctx_ref[rows, :], lat_ref[rows, :])


def _stream_shape(stream):
    if not isinstance(stream, tuple):
        return stream.shape
    return stream[0].shape[0] + stream[1].shape[0], stream[0].shape[1]


def _stream_args(stream):
    return list(stream) if isinstance(stream, tuple) else [stream]


def _resident(arr, layer):
    return _layer_block(arr, layer, pipeline_mode=pl.Buffered(1))


def _mod_kernel(c_ref, w_ref, b_ref, o_ref):
    s = _silu(c_ref[...]).astype(_BF16)
    o_ref[0] = _dot(s, w_ref[0].astype(_BF16)) + b_ref[0]


def _mod_call(cvec, w_mod, b_mod):
    depth, d, n = w_mod.shape
    tn = 1024
    return pl.pallas_call(
        _mod_kernel,
        grid=(depth, n // tn),
        in_specs=[
            pl.BlockSpec((ROUTE_ROWS, d), lambda l, j: (0, 0)),
            pl.BlockSpec((1, d, tn), lambda l, j: (l, 0, j)),
            pl.BlockSpec((1, 1, tn), lambda l, j: (l, 0, j)),
        ],
        out_specs=pl.BlockSpec((1, ROUTE_ROWS, tn), lambda l, j: (l, 0, j)),
        out_shape=jax.ShapeDtypeStruct((depth, ROUTE_ROWS, n), _F32),
        compiler_params=_params(("parallel", "parallel")),
        name="adaln_mod",
    )(cvec, w_mod, b_mod.reshape(depth, 1, n))


def _scan_masks(direction):
    row = lax.broadcasted_iota(jnp.int32, (CHUNK, CHUNK), 0)
    col = lax.broadcasted_iota(jnp.int32, (CHUNK, CHUNK), 1)
    return ((col <= row), CHUNK - 1) if direction == 0 else ((col >= row), 0)


def _inproj_kernel(*refs, n_stream):
    x_refs = refs[:n_stream]
    (mt_ref, g_ref, w_ref, wa_ref, wd_ref, bd_ref,
     uv_ref, gate_ref, v_ref, qf_ref, kf_ref, qb_ref, kb_ref, cols_ref, h_scr) = refs[n_stream:]
    tm = x_refs[0].shape[0]
    a_w = A_HEADS * A_HEAD_DIM
    k_w = B_HEADS * B_DK
    v_w = B_HEADS * B_DV
    for s in range(tm // MOD_BLOCK):
        rows = slice(s * MOD_BLOCK, (s + 1) * MOD_BLOCK)
        mt = mt_ref[s]
        h = _rms(_stream_rows(x_refs, rows), g_ref[...]) * (1.0 + mt[1:2]) + mt[0:1]
        h_scr[rows, :] = h.astype(_BF16)
    h = h_scr[...]
    off_q = 2 * a_w
    off_gate = off_q + k_w
    off_k = off_gate + v_w
    off_v = off_k + k_w
    q = _dot(h, w_ref[:, off_q:off_q + k_w]) * (B_DK ** -0.5)
    k = _dot(h, w_ref[:, off_k:off_k + k_w])
    a = _dot(h, wa_ref[...])
    sub = lax.broadcasted_iota(jnp.int32, (CHUNK, k_w), 0)
    factor_rows = [jnp.zeros((CHUNK, k_w), _F32) for _ in range(tm // CHUNK)]
    for d, (q_ref, k_ref) in enumerate(((qf_ref, kf_ref), (qb_ref, kb_ref))):
        scanned, end = _scan_masks(d)
        tri = jnp.where(scanned, 1.0, 0.0).astype(_BF16)
        la = _log_sigmoid(_dot_f32(a, wd_ref[d]) + bd_ref[d]) * (1.0 / DECAY_TAU)
        la_hi, la_lo = _split(la)
        for c in range(tm // CHUNK):
            rows = slice(c * CHUNK, (c + 1) * CHUNK)
            b = _dot(tri, la_hi[rows]) + _dot(tri, la_lo[rows])
            b_mid = b[CHUNK // 2:CHUNK // 2 + 1]
            b_end = b[end:end + 1]
            q_ref[rows, :] = (q[rows] * jnp.exp(b - b_mid)).astype(_BF16)
            kd = k[rows] * jnp.exp(b_mid - b)
            for hd in range(B_HEADS):
                kc = slice(hd * B_DK, (hd + 1) * B_DK)
                k_ref[c, kc, :] = kd[:, kc].T.astype(_BF16)
            for r, factor in enumerate((jnp.exp(b_mid), jnp.exp(b_end - b_mid), jnp.exp(b_end))):
                factor_rows[c] = jnp.where(sub == 3 * d + r, factor, factor_rows[c])
    for c in range(tm // CHUNK):
        for hd in range(B_HEADS):
            kc = slice(hd * B_DK, (hd + 1) * B_DK)
            cols_ref[c, kc, :] = factor_rows[c][:, kc].T
    for c0 in range(0, 2 * a_w, a_w):
        uv_ref[:, c0:c0 + a_w] = _dot(h, w_ref[:, c0:c0 + a_w]).astype(_BF16)
    gate_ref[...] = _dot(h, w_ref[:, off_gate:off_gate + v_w]).astype(_BF16)
    v_ref[...] = _dot(h, w_ref[:, off_v:off_v + v_w]).astype(_BF16)


def _inproj_call(layer, x, modtab, g_pre, w_main, w_a, wd, bd):
    n, d = _stream_shape(x)
    tm = TOKEN_TILE
    a_w = A_HEADS * A_HEAD_DIM
    k_w = B_HEADS * B_DK
    v_w = B_HEADS * B_DV
    tok = lambda w: pl.BlockSpec((tm, w), lambda i: (i, 0))
    tok_out = lambda w: (tok(w), jax.ShapeDtypeStruct((n, w), _BF16))
    per_chunk = lambda dtype: (pl.BlockSpec((tm // CHUNK, k_w, CHUNK), lambda i: (i, 0, 0)),
                               jax.ShapeDtypeStruct((n // CHUNK, k_w, CHUNK), dtype))
    outs = [tok_out(2 * a_w), tok_out(v_w), tok_out(v_w), tok_out(k_w), per_chunk(_BF16), tok_out(k_w),
            per_chunk(_BF16), per_chunk(_F32)]
    stream_specs = _stream_specs(x, tm)
    return pl.pallas_call(
        functools.partial(_inproj_kernel, n_stream=len(stream_specs)),
        grid=(n // tm,),
        in_specs=stream_specs + [
            _layer_block(modtab, layer, (tm // MOD_BLOCK, 8, d), lambda i: (i, 0, 0)),
            _layer_block(g_pre, layer),
            _resident(w_main, layer),
            _resident(w_a, layer),
            _layer_block(wd, layer),
            _layer_block(bd, layer),
        ],
        out_specs=[spec for spec, _ in outs],
        out_shape=[shape for _, shape in outs],
        scratch_shapes=[pltpu.VMEM((tm, d), _BF16)],
        compiler_params=_params(("parallel",)),
        name="in_proj",
    )(*_stream_args(x), modtab, g_pre, w_main, w_a, wd, bd)


def _gmlp_kernel(uv_ref, gv_ref, ws_ref, bs_ref, y_ref):
    tm = uv_ref.shape[0]
    a_w = A_HEADS * A_HEAD_DIM
    for c in range(tm // CHUNK):
        rows = slice(c * CHUNK, (c + 1) * CHUNK)
        u = _gelu(uv_ref[rows, :a_w].astype(_F32))
        v = _rms(_gelu(uv_ref[rows, a_w:].astype(_F32)), gv_ref[...]).astype(_BF16)
        for h in range(A_HEADS):
            cols = slice(h * A_HEAD_DIM, (h + 1) * A_HEAD_DIM)
            s = _dot(ws_ref[h], v[:, cols]) + bs_ref[h]
            y_ref[rows, cols] = (u[:, cols] * s).astype(_BF16)


def _gmlp_call(layer, uv, g_v, w_s, b_s):
    n = uv.shape[0]
    tm = TOKEN_TILE
    a_w = A_HEADS * A_HEAD_DIM
    return pl.pallas_call(
        _gmlp_kernel,
        grid=(n // tm,),
        in_specs=[
            pl.BlockSpec((tm, 2 * a_w), lambda i: (i, 0)),
            _layer_block(g_v, layer),
            _layer_block(w_s, layer),
            _layer_block(b_s, layer),
        ],
        out_specs=pl.BlockSpec((tm, a_w), lambda i: (i, 0)),
        out_shape=jax.ShapeDtypeStruct((n, a_w), _BF16),
        compiler_params=_params(("parallel",)),
        name="gmlp",
    )(uv, g_v, w_s, b_s)


def _gla_kernel(qf_ref, kf_ref, vf_ref, cf_ref, qb_ref, kb_ref, vb_ref, cb_ref, of_ref, ob_ref, s_ref):
    @pl.when(pl.program_id(1) == 0)
    def _():
        s_ref[...] = jnp.zeros_like(s_ref)

    group = qf_ref.shape[0] // CHUNK
    streams = ((qf_ref, kf_ref, vf_ref, cf_ref, of_ref), (qb_ref, kb_ref, vb_ref, cb_ref, ob_ref))
    for g in range(group):
        for d, (q_ref, k_ref, v_ref, c_ref, o_ref) in enumerate(streams):
            scanned, _ = _scan_masks(d)
            c = g if d == 0 else group - 1 - g
            rows = slice(c * CHUNK, (c + 1) * CHUNK)
            for h in range(B_HEADS):
                kc = slice(h * B_DK, (h + 1) * B_DK)
                vc = slice(h * B_DV, (h + 1) * B_DV)
                cols = c_ref[c, kc, :]
                e_mid, e_end, decay = (cols[:, 3 * d + r:3 * d + r + 1] for r in range(3))
                qd = q_ref[rows, kc]
                kd_t = k_ref[c, kc, :]
                vh = v_ref[rows, vc]
                state = s_ref[d, h]
                scores = jnp.where(scanned, _dot(qd, kd_t), 0.0).astype(_BF16)
                o_ref[rows, vc] = _dot(scores, vh) + _dot(qd, (e_mid * state).astype(_BF16))
                s_ref[d, h] = decay * state + e_end * _dot(kd_t, vh)


def _gla_call(qf, kf, qb, kb, v, cols, n_batch, ctx_chunks, lat_chunks):
    n = v.shape[0]
    k_w = B_HEADS * B_DK
    v_w = B_HEADS * B_DV
    group = GLA_GROUP
    assert ctx_chunks % group == 0 and lat_chunks % group == 0
    ctx_steps = ctx_chunks // group
    lat_steps = lat_chunks // group
    steps = ctx_steps + lat_steps
    lat0 = n_batch * ctx_steps

    def fwd(b, t):
        return jnp.where(t < ctx_steps, b * ctx_steps + t, lat0 + b * lat_steps + (t - ctx_steps))

    def bwd(b, t):
        return jnp.where(t < ctx_steps, b * ctx_steps + (ctx_steps - 1 - t),
                         lat0 + b * lat_steps + (steps - 1 - t))

    def specs(group_of):
        tok = lambda w: pl.BlockSpec((group * CHUNK, w), lambda b, t: (group_of(b, t), 0))
        per_chunk = pl.BlockSpec((group, k_w, CHUNK), lambda b, t: (group_of(b, t), 0, 0))
        return [tok(k_w), per_chunk, tok(v_w), per_chunk]

    return pl.pallas_call(
        _gla_kernel,
        grid=(n_batch, steps),
        in_specs=specs(fwd) + specs(bwd),
        out_specs=[pl.BlockSpec((group * CHUNK, v_w), lambda b, t: (fwd(b, t), 0)),
                   pl.BlockSpec((group * CHUNK, v_w), lambda b, t: (bwd(b, t), 0))],
        out_shape=[jax.ShapeDtypeStruct((n, v_w), _F32)] * 2,
        scratch_shapes=[pltpu.VMEM((2, B_HEADS, B_DK, B_DV), _F32)],
        compiler_params=_params(("parallel", "arbitrary")),
        name="gla_scan",
    )(qf, kf, v, cols, qb, kb, v, cols)


def _outproj_kernel(*refs, routed, n_stream):
    x_refs, refs = refs[:n_stream], refs[n_stream:]
    if routed:
        (ya_ref, of_ref, ob_ref, gate_ref, mt_ref, gn_ref, gpost_ref, gffn_ref, w_ref, wr_ref,
         xo_ref, f_ref, route_ref, y_scr, f_scr) = refs
    else:
        (ya_ref, of_ref, ob_ref, gate_ref, mt_ref, gn_ref, gpost_ref, gffn_ref, w_ref,
         xo_ref, f_ref, y_scr) = refs
    a_w = A_HEADS * A_HEAD_DIM
    tm = x_refs[0].shape[0]
    for s in range(tm // MOD_BLOCK):
        rows = slice(s * MOD_BLOCK, (s + 1) * MOD_BLOCK)
        y_scr[rows, :a_w] = ya_ref[rows, :]
        for h in range(B_HEADS):
            vc = slice(h * B_DV, (h + 1) * B_DV)
            o = of_ref[rows, vc] + ob_ref[rows, vc]
            o = o * lax.rsqrt(jnp.mean(o * o, axis=-1, keepdims=True) + EPS) * gn_ref[:, vc]
            y_scr[rows, a_w + h * B_DV:a_w + (h + 1) * B_DV] = (
                o * _silu(gate_ref[rows, vc].astype(_F32))).astype(_BF16)
        m = _rms(_dot(y_scr[rows, :], w_ref[...]), gpost_ref[...])
        mt = mt_ref[s]
        x = _stream_rows(x_refs, rows) + mt[2:3] * m
        xo_ref[rows, :] = x
        f = _rms(x, gffn_ref[...]) * (1.0 + mt[4:5]) + mt[3:4]
        if routed:
            f_scr[rows, :] = f
            half = f.shape[1] // 2
            f_ref[rows, :] = _pack_bf16_pair(f[:, :half], f[:, half:])
        else:
            f_ref[rows, :] = f.astype(_BF16)
    if routed:
        logits = _dot_f32(wr_ref[...], f_scr[...], _dot_nt)
        eidx = lax.broadcasted_iota(jnp.int32, logits.shape, 0).astype(_F32)
        none = float(N_EXPERTS)
        m1 = jnp.max(logits, axis=0, keepdims=True)
        i1 = jnp.min(jnp.where(logits == m1, eidx, none), axis=0, keepdims=True)
        rest = jnp.where(eidx == i1, -jnp.inf, logits)
        m2 = jnp.max(rest, axis=0, keepdims=True)
        i2 = jnp.min(jnp.where(rest == m2, eidx, none), axis=0, keepdims=True)
        e2 = jnp.exp(m2 - m1)
        w1 = 1.0 / (1.0 + e2)
        w2 = e2 / (1.0 + e2)
        r = lax.broadcasted_iota(jnp.int32, logits.shape, 0)
        route_ref[...] = jnp.where(r == 0, i1, jnp.where(r == 1, i2, jnp.where(r == 2, w1, jnp.where(r == 3, w2, 0.0))))


def _outproj_call(layer, ya, o_f, o_b, gate, x, modtab, g_norm, g_post, g_ffn, w_out, wr_t, route_layer):
    n, d = _stream_shape(x)
    tm = TOKEN_TILE
    routed = wr_t is not None
    tok = lambda w: pl.BlockSpec((tm, w), lambda i: (i, 0))
    a_w = A_HEADS * A_HEAD_DIM
    v_w = B_HEADS * B_DV
    stream_specs = _stream_specs(x, tm)
    in_specs = stream_specs + [
                tok(a_w), tok(v_w), tok(v_w), tok(v_w),
                _layer_block(modtab, layer, (tm // MOD_BLOCK, 8, d), lambda i: (i, 0, 0)),
                _layer_block(g_norm, layer), _layer_block(g_post, layer), _layer_block(g_ffn, layer),
                _resident(w_out, layer)]
    args = _stream_args(x) + [ya, o_f, o_b, gate, modtab, g_norm, g_post, g_ffn, w_out]
    scratch = [pltpu.VMEM((tm, a_w + v_w), _BF16)]
    out_specs = [tok(d), tok(d // 2 if routed else d)]
    out_shape = [jax.ShapeDtypeStruct((n, d), _F32),
                 jax.ShapeDtypeStruct((n, d // 2), jnp.uint32) if routed else jax.ShapeDtypeStruct((n, d), _BF16)]
    if routed:
        scratch.append(pltpu.VMEM((tm, d), _F32))
        in_specs.append(_layer_block(wr_t, route_layer))
        args.append(wr_t)
        out_specs.append(pl.BlockSpec((ROUTE_ROWS, tm), lambda i: (0, i)))
        out_shape.append(jax.ShapeDtypeStruct((ROUTE_ROWS, n), _F32))
    return pl.pallas_call(
        functools.partial(_outproj_kernel, routed=routed, n_stream=len(stream_specs)),
        grid=(n // tm,),
        in_specs=in_specs,
        out_specs=out_specs,
        out_shape=out_shape,
        scratch_shapes=scratch,
        compiler_params=_params(("parallel",)),
        name="out_proj_routed" if routed else "out_proj",
    )(*args)


def _ffn_kernel(f_ref, wg_ref, wu_ref, wd_ref, x_ref, mt_ref, g_ref, o_ref):
    j = pl.program_id(1)

    @pl.when(j == 0)
    def _():
        o_ref[...] = jnp.zeros_like(o_ref)

    f = f_ref[...]
    act = (_silu(_dot(f, wg_ref[...].astype(_BF16))) * _dot(f, wu_ref[...].astype(_BF16))).astype(_BF16)
    o_ref[...] += _dot(act, wd_ref[...].astype(_BF16))

    @pl.when(j == pl.num_programs(1) - 1)
    def _():
        for s in range(o_ref.shape[0] // MOD_BLOCK):
            rows = slice(s * MOD_BLOCK, (s + 1) * MOD_BLOCK)
            o_ref[rows, :] = x_ref[rows, :] + mt_ref[s][5:6] * _rms(o_ref[rows, :], g_ref[...])


def _token_tile(n, target):
    tm = target - target % MOD_BLOCK
    while n % tm:
        tm -= MOD_BLOCK
    return tm


def _ffn_call(layer, ffn_layer, f, w_gate, w_up, w_down, x, modtab, g_post):
    n, d = x.shape
    d_ff = w_gate.shape[2]
    tm = _token_tile(n, FFN_TOKEN_TILE)
    tf = FF_TILE
    return pl.pallas_call(
        _ffn_kernel,
        grid=(n // tm, d_ff // tf),
        in_specs=[
            pl.BlockSpec((tm, d), lambda i, j: (i, 0)),
            _layer_block(w_gate, ffn_layer, (d, tf), lambda i, j: (0, j)),
            _layer_block(w_up, ffn_layer, (d, tf), lambda i, j: (0, j)),
            _layer_block(w_down, ffn_layer, (tf, d), lambda i, j: (j, 0)),
            pl.BlockSpec((tm, d), lambda i, j: (i, 0)),
            _layer_block(modtab, layer, (tm // MOD_BLOCK, 8, d), lambda i, j: (i, 0, 0)),
            _layer_block(g_post, layer),
        ],
        out_specs=pl.BlockSpec((tm, d), lambda i, j: (i, 0)),
        out_shape=jax.ShapeDtypeStruct((n, d), _F32),
        compiler_params=_params(("parallel", "arbitrary")),
        name="dense_ffn",
    )(f, w_gate, w_up, w_down, x, modtab, g_post)


def _rank_kernel(route_ref, rank_ref, count_ref, carry):
    @pl.when(pl.program_id(0) == 0)
    def _():
        carry[...] = jnp.zeros_like(carry)

    tl = route_ref.shape[1]
    r = route_ref[...]
    eidx = lax.broadcasted_iota(jnp.int32, r.shape, 0).astype(_F32)
    oh1 = jnp.where(eidx == r[0:1], 1.0, 0.0)
    oh2 = jnp.where(eidx == r[1:2], 1.0, 0.0)
    both = oh1 + oh2
    before = lax.broadcasted_iota(jnp.int32, (tl, tl), 0) < lax.broadcasted_iota(jnp.int32, (tl, tl), 1)
    seen = carry[:, 0:1] + _dot(both.astype(_BF16), jnp.where(before, 1.0, 0.0).astype(_BF16))
    rank1 = jnp.sum(oh1 * seen, axis=0, keepdims=True)
    rank2 = jnp.sum(oh2 * seen, axis=0, keepdims=True)
    row = lax.broadcasted_iota(jnp.int32, r.shape, 0)
    rank_ref[...] = jnp.where(row == 0, rank1, jnp.where(row == 1, rank2, 0.0))
    carry[...] = carry[...] + jnp.sum(both, axis=1, keepdims=True)
    count_ref[...] = carry[...]


def _rank_call(route):
    n = route.shape[1]
    tl = TOKEN_TILE
    return pl.pallas_call(
        _rank_kernel,
        grid=(n // tl,),
        in_specs=[pl.BlockSpec((ROUTE_ROWS, tl), lambda i: (0, i))],
        out_specs=[pl.BlockSpec((ROUTE_ROWS, tl), lambda i: (0, i)),
                   pl.BlockSpec((ROUTE_ROWS, 128), lambda i: (0, 0))],
        out_shape=[jax.ShapeDtypeStruct((ROUTE_ROWS, n), _F32),
                   jax.ShapeDtypeStruct((ROUTE_ROWS, 128), _F32)],
        scratch_shapes=[pltpu.VMEM((ROUTE_ROWS, 128), _F32)],
        compiler_params=_params(("arbitrary",)),
        name="moe_rank",
    )(route)


def _dispatch_kernel(dest_ref, f_ref, xs_in_ref, xs_ref, sem):
    del xs_in_ref
    td = f_ref.shape[0]

    def row_copy(t, k):
        return pltpu.make_async_copy(f_ref.at[pl.ds(t, 1)], xs_ref.at[pl.ds(dest_ref[0, k * td + t], 1)], sem)

    def issue(t, carry):
        row_copy(t, 0).start()
        row_copy(t, 1).start()
        return carry

    lax.fori_loop(0, td, issue, 0, unroll=DMA_ISSUE_UNROLL)
    for _ in range(2):
        pltpu.make_async_copy(f_ref, xs_ref.at[pl.ds(0, td)], sem).wait()


def _dispatch_call(dest_blocks, f, xs_init):
    n_steps = dest_blocks.shape[0]
    return pl.pallas_call(
        _dispatch_kernel,
        grid=(n_steps,),
        in_specs=[
            pl.BlockSpec((None, 1, dest_blocks.shape[2]), lambda i: (i, 0, 0), memory_space=pltpu.SMEM),
            pl.BlockSpec((dest_blocks.shape[2] // 2, f.shape[1]), lambda i: (i, 0)),
            pl.BlockSpec(memory_space=pl.ANY),
        ],
        out_specs=pl.BlockSpec(memory_space=pl.ANY),
        out_shape=jax.ShapeDtypeStruct(xs_init.shape, xs_init.dtype),
        scratch_shapes=[pltpu.SemaphoreType.DMA(())],
        input_output_aliases={2: 0},
        compiler_params=_params(("arbitrary",)),
        name="moe_dispatch",
    )(dest_blocks, f, xs_init)


def _expert_kernel(be_ref, rows_ref, xs_ref, wg_ref, wu_ref, wd_ref, ys_ref, xb_scr):
    del be_ref
    n_rows = rows_ref[pl.program_id(0)]

    @pl.when(pl.program_id(1) == 0)
    def _():
        ys_ref[...] = jnp.zeros_like(ys_ref)
        half = xs_ref.shape[1]
        xb_scr[:, :half], xb_scr[:, half:] = _unpack_bf16_pair(xs_ref[...])

    def swiglu_pieces(*pieces):
        wg, wu, wd = (ref[...].astype(_BF16) for ref in (wg_ref, wu_ref, wd_ref))
        for rows in pieces:
            xb = xb_scr[rows, :]
            act = (_silu(_dot(xb, wg)) * _dot(xb, wu)).astype(_BF16)
            ys_ref[rows, :] += _dot(act, wd)

    halves = [slice(b, b + EXPERT_HALF_ROWS) for b in range(0, EXPERT_ROWS, EXPERT_HALF_ROWS)]
    pl.when(n_rows == EXPERT_ROWS)(functools.partial(swiglu_pieces, *halves))
    for rows in halves:
        base = rows.start
        pl.when((n_rows >= base + EXPERT_HALF_ROWS) & (n_rows < EXPERT_ROWS))(functools.partial(swiglu_pieces, rows))
        for part in range(EXPERT_SUB_ROWS, EXPERT_HALF_ROWS, EXPERT_SUB_ROWS):
            pl.when(n_rows == base + part)(functools.partial(swiglu_pieces, slice(base, base + part)))


def _expert_call(moe_layer, block_expert, block_rows, xs, w_gate, w_up, w_down):
    cap = xs.shape[0]
    d, d_ff = w_gate.shape[2:]
    tb = EXPERT_ROWS
    tf = FF_TILE
    n_ff = d_ff // tf

    def ff_of(i, j, rows):
        return jnp.where(rows[i] > 0, j, n_ff - 1)

    grid_spec = pltpu.PrefetchScalarGridSpec(
        num_scalar_prefetch=2,
        grid=(cap // tb, n_ff),
        in_specs=[
            pl.BlockSpec((tb, xs.shape[1]), lambda i, j, be, rows: (i, 0)),
            _layer_block(w_gate, moe_layer, (None, d, tf), lambda i, j, be, rows: (be[i], 0, ff_of(i, j, rows))),
            _layer_block(w_up, moe_layer, (None, d, tf), lambda i, j, be, rows: (be[i], 0, ff_of(i, j, rows))),
            _layer_block(w_down, moe_layer, (None, tf, d), lambda i, j, be, rows: (be[i], ff_of(i, j, rows), 0)),
        ],
        out_specs=pl.BlockSpec((tb, d), lambda i, j, be, rows: (i, 0)),
        scratch_shapes=[pltpu.VMEM((tb, d), _BF16)],
    )
    return pl.pallas_call(
        _expert_kernel,
        grid_spec=grid_spec,
        out_shape=jax.ShapeDtypeStruct((cap, d), _F32),
        compiler_params=_params(("parallel", "arbitrary")),
        name="moe_experts",
    )(block_expert, block_rows, xs, w_gate, w_up, w_down)


def _combine_kernel(dest_ref, dest_next_ref, ys_ref, w_ref, x_ref, mt_ref, g_ref, o_ref, buf, sems, *, skip_blocks):
    tc = x_ref.shape[0]
    i = pl.program_id(0)
    cur = i % 2

    def gather(idx_ref, b):
        def row_copy(t, k):
            return pltpu.make_async_copy(ys_ref.at[pl.ds(idx_ref[0, k * tc + t], 1)], buf.at[b, k, pl.ds(t, 1)], sems.at[b])

        def issue(t, carry):
            row_copy(t, 0).start()
            row_copy(t, 1).start()
            return carry

        lax.fori_loop(0, tc, issue, 0, unroll=DMA_ISSUE_UNROLL)

    pl.when(i == skip_blocks)(lambda: gather(dest_ref, cur))
    pl.when((i >= skip_blocks) & (i < pl.num_programs(0) - 1))(lambda: gather(dest_next_ref, 1 - cur))

    @pl.when(i >= skip_blocks)
    def _():
        for k in range(2):
            pltpu.make_async_copy(ys_ref.at[pl.ds(0, tc)], buf.at[cur, k], sems.at[cur]).wait()
        w = w_ref[...]
        f = w[:, 0:1] * buf[cur, 0] + w[:, 1:2] * buf[cur, 1]
        for s in range(tc // MOD_BLOCK):
            rows = slice(s * MOD_BLOCK, (s + 1) * MOD_BLOCK)
            o_ref[rows, :] = x_ref[rows, :] + mt_ref[s][5:6] * _rms(f[rows, :], g_ref[...])


def _combine_call(layer, dest_blocks, ys, w_cols, x, modtab, g_post, skip_rows):
    n, d = x.shape
    tc = COMBINE_ROWS
    skip_blocks = skip_rows // tc
    return pl.pallas_call(
        functools.partial(_combine_kernel, skip_blocks=skip_blocks),
        grid=(n // tc,),
        in_specs=[
            pl.BlockSpec((None, 1, 2 * tc), lambda i: (i, 0, 0), memory_space=pltpu.SMEM),
            pl.BlockSpec((None, 1, 2 * tc), lambda i: (jnp.minimum(i + 1, n // tc - 1), 0, 0),
                         memory_space=pltpu.SMEM),
            pl.BlockSpec(memory_space=pl.ANY),
            pl.BlockSpec((tc, 2), lambda i: (i, 0)),
            pl.BlockSpec((tc, d), lambda i: (i, 0)),
            _layer_block(modtab, layer, (tc // MOD_BLOCK, 8, d), lambda i: (i, 0, 0)),
            _layer_block(g_post, layer),
        ],
        out_specs=pl.BlockSpec((tc, d), lambda i: (jnp.maximum(i - skip_blocks, 0), 0)),
        out_shape=jax.ShapeDtypeStruct((n - skip_rows, d), _F32),
        scratch_shapes=[pltpu.VMEM((2, 2, tc, d), _F32), pltpu.SemaphoreType.DMA((2,))],
        compiler_params=_params(("arbitrary",)),
        name="moe_combine",
    )(dest_blocks, dest_blocks, ys, w_cols, x, modtab, g_post)


def _dest_blocks(dest, rows):
    n = dest.shape[1]
    return dest.reshape(2, n // rows, rows).transpose(1, 0, 2).reshape(n // rows, 1, 2 * rows)


def _moe_call(layer, moe_layer, f, route, x, modtab, g_post, w_gate, w_up, w_down, skip_rows):
    n, d = x.shape
    tb = EXPERT_ROWS
    rank, counts = _rank_call(route)
    expert = route[0:2].astype(jnp.int32)
    counts = counts[:, 0].astype(jnp.int32)
    padded = (counts + tb - 1) // tb * tb
    pad_end = jnp.cumsum(padded)
    pad_start = pad_end - padded
    dest = rank[0:2].astype(jnp.int32) + sum(jnp.where(expert == e, pad_start[e], 0) for e in range(N_EXPERTS))
    n_blocks = -(-(2 * n + N_EXPERTS * (tb - 1)) // tb)
    block_start = jnp.arange(n_blocks, dtype=jnp.int32) * tb
    block_expert = jnp.minimum(jnp.sum(block_start[:, None] >= pad_end[None, :], axis=1), N_EXPERTS - 1).astype(jnp.int32)
    sub = EXPERT_SUB_ROWS
    used_end = pad_start + (counts + sub - 1) // sub * sub
    block_used_end = sum(jnp.where(block_expert == e, used_end[e], 0) for e in range(N_EXPERTS))
    block_rows = jnp.clip(block_used_end - block_start, 0, tb).astype(jnp.int32)
    xs = _dispatch_call(_dest_blocks(dest, TOKEN_TILE), f, jnp.zeros((n_blocks * tb, f.shape[1]), f.dtype))
    ys = _expert_call(moe_layer, block_expert, block_rows, xs, w_gate, w_up, w_down)
    return _combine_call(layer, _dest_blocks(dest, COMBINE_ROWS), ys, route[2:4].T, x, modtab, g_post, skip_rows)


def kernel(x, c, ctx, c_ctx, w_mod, b_mod, g_pre_mix, g_post_mix, g_pre_ffn, g_post_ffn, w_in, w_out, g_vnorm, w_spatial, b_spatial, w_decay, b_decay, g_gla_norm, w_ffn_gate, w_ffn_up, w_ffn_down, w_router, w_exp_gate, w_exp_up, w_exp_down):
    n_batch, seq, d = x.shape
    ctx_len = ctx.shape[1]
    depth = w_mod.shape[0]
    a_w = A_HEADS * A_HEAD_DIM
    k_w = B_HEADS * B_DK
    v_w = B_HEADS * B_DV
    main_w = 2 * a_w + 2 * k_w + 2 * v_w
    assert w_in.shape[2] == main_w + 2 * DECAY_RANK and seq % TOKEN_TILE == 0
    assert (n_batch * ctx_len) % TOKEN_TILE == 0 and ctx_len % MOD_BLOCK == 0

    stream = (ctx.reshape(n_batch * ctx_len, d), x.reshape(n_batch * seq, d))

    cvec = jnp.zeros((ROUTE_ROWS, d), _F32).at[:n_batch].set(c).at[n_batch].set(c_ctx)
    mod = _mod_call(cvec, w_mod, b_mod).reshape(depth, ROUTE_ROWS, 6, d)
    per_block = lambda r, blocks: jnp.broadcast_to(mod[:, r:r + 1], (depth, blocks, 6, d))
    modtab = jnp.concatenate([per_block(n_batch, n_batch * ctx_len // MOD_BLOCK)]
                             + [per_block(b, seq // MOD_BLOCK) for b in range(n_batch)], axis=1)
    modtab = jnp.pad(modtab, ((0, 0), (0, 0), (0, 2), (0, 0)))

    w_main = w_in.astype(_BF16)
    w_a = jnp.pad(w_in[:, :, main_w:].astype(_BF16), ((0, 0), (0, 0), (0, CHUNK - 2 * DECAY_RANK)))
    wd = jnp.stack([jnp.pad(w_decay[:, s], ((0, 0), (s * DECAY_RANK, CHUNK - (s + 1) * DECAY_RANK), (0, 0)))
                    for s in range(2)], axis=1)
    bd = b_decay[:, :, None, :]
    w_out_b = w_out.astype(_BF16)
    w_s = w_spatial.astype(_BF16)
    b_s = b_spatial[:, :, :, None]
    wr_t = w_router.transpose(0, 2, 1)
    rows = lambda g: g[:, None, :]
    g_pre_mix, g_post_mix, g_pre_ffn, g_post_ffn, g_vnorm, g_gla_norm = map(
        rows, (g_pre_mix, g_post_mix, g_pre_ffn, g_post_ffn, g_vnorm, g_gla_norm))

    n_ctx = n_batch * ctx_len
    for l in range(depth):
        last = l == depth - 1
        uv, gate, v, qf, kf, qb, kb, cols = _inproj_call(l, stream, modtab, g_pre_mix, w_main, w_a, wd, bd)
        ya = _gmlp_call(l, uv, g_vnorm, w_s, b_s)
        o_f, o_b = _gla_call(qf, kf, qb, kb, v, cols, n_batch, ctx_len // CHUNK, seq // CHUNK)
        routed = l % 2 == 1
        outs = _outproj_call(l, ya, o_f, o_b, gate, stream, modtab, g_gla_norm, g_post_mix, g_pre_ffn, w_out_b,
                             wr_t if routed else None, l // 2)
        if routed:
            stream, f, route = outs
            stream = _moe_call(l, l // 2, f, route, stream, modtab, g_post_ffn, w_exp_gate, w_exp_up, w_exp_down,
                               n_ctx if last else 0)
        else:
            stream, f = outs
            stream = _ffn_call(l, l // 2, f, w_ffn_gate, w_ffn_up, w_ffn_down, stream, modtab, g_post_ffn)
            if last:
                stream = stream[n_ctx:]
    return stream.reshape(n_batch, seq, d)
```

```python
import functools

import jax
import jax.numpy as jnp
from jax import lax
from jax.experimental import pallas as pl
from jax.experimental.pallas import tpu as pltpu

_F32 = jnp.float32
_BF16 = jnp.bfloat16

EPS = 1e-6
CHUNK = 128
GLA_GROUP = 2
MOD_BLOCK = 256
A_HEADS = 8
A_HEAD_DIM = 128
B_HEADS = 4
B_DK = 128
B_DV = 256
DECAY_RANK = 16
DECAY_TAU = 16.0
N_EXPERTS = 8
ROUTE_ROWS = 8
TOKEN_TILE = 512
FFN_TOKEN_TILE = 768
FF_TILE = 256
EXPERT_ROWS = 1536
EXPERT_HALF_ROWS = 512
EXPERT_SUB_ROWS = 256
COMBINE_ROWS = 256
DMA_ISSUE_UNROLL = 8
VMEM_LIMIT = 56 << 20
EXPERT_VMEM_LIMIT = 60 << 20


def _dot(a, b):
    return jnp.dot(a, b, preferred_element_type=_F32)


def _dot_nt(a, b):
    return lax.dot_general(a, b, (((1,), (1,)), ((), ())), preferred_element_type=_F32)


def _dot_tn(a, b):
    return lax.dot_general(a, b, (((0,), (0,)), ((), ())), preferred_element_type=_F32)


def _split(x):
    hi = x.astype(_BF16)
    lo = (x - hi.astype(_F32)).astype(_BF16)
    return hi, lo


def _dot_f32(a, b, dot=_dot):
    ah, al = _split(a)
    bh, bl = _split(b)
    return dot(ah, bh) + dot(ah, bl) + dot(al, bh)


def _rms(x, g):
    return x * lax.rsqrt(jnp.mean(x * x, axis=-1, keepdims=True) + EPS) * g


def _gelu(x):
    return 0.5 * x * (1.0 + lax.erf(x * (2.0 ** -0.5)))


def _silu(x):
    return x * jax.nn.sigmoid(x)


def _log_sigmoid(z):
    return jnp.minimum(z, 0.0) - jnp.log1p(jnp.exp(-jnp.abs(z)))


def _pack_bf16_pair(left, right):
    hi = lax.bitcast_convert_type(left.astype(_BF16).astype(_F32), jnp.uint32)
    lo = lax.bitcast_convert_type(right.astype(_BF16).astype(_F32), jnp.uint32)
    return hi | (lo >> 16)


def _unpack_bf16_pair(packed):
    left = lax.bitcast_convert_type(packed & jnp.uint32(0xFFFF0000), _F32).astype(_BF16)
    right = lax.bitcast_convert_type(packed << 16, _F32).astype(_BF16)
    return left, right


def _params(semantics, vmem_limit=VMEM_LIMIT):
    return pltpu.CompilerParams(dimension_semantics=semantics, vmem_limit_bytes=vmem_limit)


def _layer_block(arr, layer, block=None, index=None, **kwargs):
    per_layer = arr.shape[1:]
    block = per_layer if block is None else block
    index = (lambda *_: (0,) * len(per_layer)) if index is None else index
    return pl.BlockSpec((None,) + tuple(block), lambda *a: (layer,) + tuple(index(*a)), **kwargs)


def _stream_specs(stream, tm):
    if not isinstance(stream, tuple):
        return [pl.BlockSpec((tm, stream.shape[1]), lambda i: (i, 0))]
    ctx, lat = stream
    assert ctx.shape[0] == tm and lat.shape[0] % tm == 0
    return [pl.BlockSpec((tm, ctx.shape[1]), lambda i: (0, 0)),
            pl.BlockSpec((tm, lat.shape[1]), lambda i: (jnp.maximum(i - 1, 0), 0))]


def _stream_rows(x_refs, rows):
    if len(x_refs) == 1:
        return x_refs[0][rows, :]
    ctx_ref, lat_ref = x_refs
    return jnp.where(pl.program_id(0) == 0, ctx_ref[rows, :], lat_ref[rows, :])


def _stream_shape(stream):
    if not isinstance(stream, tuple):
        return stream.shape
    return stream[0].shape[0] + stream[1].shape[0], stream[0].shape[1]


def _stream_args(stream):
    return list(stream) if isinstance(stream, tuple) else [stream]


def _resident(arr, layer):
    return _layer_block(arr, layer, pipeline_mode=pl.Buffered(1))


def _mod_kernel(c_ref, w_ref, b_ref, o_ref):
    s = _silu(c_ref[...]).astype(_BF16)
    o_ref[0] = _dot(s, w_ref[0].astype(_BF16)) + b_ref[0]


def _mod_call(cvec, w_mod, b_mod):
    depth, d, n = w_mod.shape
    tn = 1024
    return pl.pallas_call(
        _mod_kernel,
        grid=(depth, n // tn),
        in_specs=[
            pl.BlockSpec((ROUTE_ROWS, d), lambda l, j: (0, 0)),
            pl.BlockSpec((1, d, tn), lambda l, j: (l, 0, j)),
            pl.BlockSpec((1, 1, tn), lambda l, j: (l, 0, j)),
        ],
        out_specs=pl.BlockSpec((1, ROUTE_ROWS, tn), lambda l, j: (l, 0, j)),
        out_shape=jax.ShapeDtypeStruct((depth, ROUTE_ROWS, n), _F32),
        compiler_params=_params(("parallel", "parallel")),
        name="adaln_mod",
    )(cvec, w_mod, b_mod.reshape(depth, 1, n))


def _scan_masks(direction):
    row = lax.broadcasted_iota(jnp.int32, (CHUNK, CHUNK), 0)
    col = lax.broadcasted_iota(jnp.int32, (CHUNK, CHUNK), 1)
    return ((col <= row), CHUNK - 1) if direction == 0 else ((col >= row), 0)


def _inproj_kernel(*refs, n_stream):
    x_refs = refs[:n_stream]
    (mt_ref, g_ref, w_ref, wa_ref, wd_ref, bd_ref,
     uv_ref, gate_ref, v_ref, qf_ref, kf_ref, qb_ref, kb_ref, cols_ref, h_scr) = refs[n_stream:]
    tm = x_refs[0].shape[0]
    a_w = A_HEADS * A_HEAD_DIM
    k_w = B_HEADS * B_DK
    v_w = B_HEADS * B_DV
    for s in range(tm // MOD_BLOCK):
        rows = slice(s * MOD_BLOCK, (s + 1) * MOD_BLOCK)
        mt = mt_ref[s]
        h = _rms(_stream_rows(x_refs, rows), g_ref[...]) * (1.0 + mt[1:2]) + mt[0:1]
        h_scr[rows, :] = h.astype(_BF16)
    h = h_scr[...]
    off_q = 2 * a_w
    off_gate = off_q + k_w
    off_k = off_gate + v_w
    off_v = off_k + k_w
    q = _dot(h, w_ref[:, off_q:off_q + k_w]) * (B_DK ** -0.5)
    k = _dot(h, w_ref[:, off_k:off_k + k_w])
    a = _dot(h, wa_ref[...])
    sub = lax.broadcasted_iota(jnp.int32, (CHUNK, k_w), 0)
    factor_rows = [jnp.zeros((CHUNK, k_w), _F32) for _ in range(tm // CHUNK)]
    for d, (q_ref, k_ref) in enumerate(((qf_ref, kf_ref), (qb_ref, kb_ref))):
        scanned, end = _scan_masks(d)
        tri = jnp.where(scanned, 1.0, 0.0).astype(_BF16)
        la = _log_sigmoid(_dot_f32(a, wd_ref[d]) + bd_ref[d]) * (1.0 / DECAY_TAU)
        la_hi, la_lo = _split(la)
        for c in range(tm // CHUNK):
            rows = slice(c * CHUNK, (c + 1) * CHUNK)
            b = _dot(tri, la_hi[rows]) + _dot(tri, la_lo[rows])
            b_mid = b[CHUNK // 2:CHUNK // 2 + 1]
            b_end = b[end:end + 1]
            q_ref[rows, :] = (q[rows] * jnp.exp(b - b_mid)).astype(_BF16)
            kd = k[rows] * jnp.exp(b_mid - b)
            for hd in range(B_HEADS):
                kc = slice(hd * B_DK, (hd + 1) * B_DK)
                k_ref[c, kc, :] = kd[:, kc].T.astype(_BF16)
            for r, factor in enumerate((jnp.exp(b_mid), jnp.exp(b_end - b_mid), jnp.exp(b_end))):
                factor_rows[c] = jnp.where(sub == 3 * d + r, factor, factor_rows[c])
    for c in range(tm // CHUNK):
        for hd in range(B_HEADS):
            kc = slice(hd * B_DK, (hd + 1) * B_DK)
            cols_ref[c, kc, :] = factor_rows[c][:, kc].T
    for c0 in range(0, 2 * a_w, a_w):
        uv_ref[:, c0:c0 + a_w] = _dot(h, w_ref[:, c0:c0 + a_w]).astype(_BF16)
    gate_ref[...] = _dot(h, w_ref[:, off_gate:off_gate + v_w]).astype(_BF16)
    v_ref[...] = _dot(h, w_ref[:, off_v:off_v + v_w]).astype(_BF16)


def _inproj_call(layer, x, modtab, g_pre, w_main, w_a, wd, bd):
    n, d = _stream_shape(x)
    tm = TOKEN_TILE
    a_w = A_HEADS * A_HEAD_DIM
    k_w = B_HEADS * B_DK
    v_w = B_HEADS * B_DV
    tok = lambda w: pl.BlockSpec((tm, w), lambda i: (i, 0))
    tok_out = lambda w: (tok(w), jax.ShapeDtypeStruct((n, w), _BF16))
    per_chunk = lambda dtype: (pl.BlockSpec((tm // CHUNK, k_w, CHUNK), lambda i: (i, 0, 0)),
                               jax.ShapeDtypeStruct((n // CHUNK, k_w, CHUNK), dtype))
    outs = [tok_out(2 * a_w), tok_out(v_w), tok_out(v_w), tok_out(k_w), per_chunk(_BF16), tok_out(k_w),
            per_chunk(_BF16), per_chunk(_F32)]
    stream_specs = _stream_specs(x, tm)
    return pl.pallas_call(
        functools.partial(_inproj_kernel, n_stream=len(stream_specs)),
        grid=(n // tm,),
        in_specs=stream_specs + [
            _layer_block(modtab, layer, (tm // MOD_BLOCK, 8, d), lambda i: (i, 0, 0)),
            _layer_block(g_pre, layer),
            _resident(w_main, layer),
            _resident(w_a, layer),
            _layer_block(wd, layer),
            _layer_block(bd, layer),
        ],
        out_specs=[spec for spec, _ in outs],
        out_shape=[shape for _, shape in outs],
        scratch_shapes=[pltpu.VMEM((tm, d), _BF16)],
        compiler_params=_params(("parallel",)),
        name="in_proj",
    )(*_stream_args(x), modtab, g_pre, w_main, w_a, wd, bd)


def _gmlp_kernel(uv_ref, gv_ref, ws_ref, bs_ref, y_ref):
    tm = uv_ref.shape[0]
    a_w = A_HEADS * A_HEAD_DIM
    for c in range(tm // CHUNK):
        rows = slice(c * CHUNK, (c + 1) * CHUNK)
        u = _gelu(uv_ref[rows, :a_w].astype(_F32))
        v = _rms(_gelu(uv_ref[rows, a_w:].astype(_F32)), gv_ref[...]).astype(_BF16)
        for h in range(A_HEADS):
            cols = slice(h * A_HEAD_DIM, (h + 1) * A_HEAD_DIM)
            s = _dot(ws_ref[h], v[:, cols]) + bs_ref[h]
            y_ref[rows, cols] = (u[:, cols] * s).astype(_BF16)


def _gmlp_call(layer, uv, g_v, w_s, b_s):
    n = uv.shape[0]
    tm = TOKEN_TILE
    a_w = A_HEADS * A_HEAD_DIM
    return pl.pallas_call(
        _gmlp_kernel,
        grid=(n // tm,),
        in_specs=[
            pl.BlockSpec((tm, 2 * a_w), lambda i: (i, 0)),
            _layer_block(g_v, layer),
            _layer_block(w_s, layer),
            _layer_block(b_s, layer),
        ],
        out_specs=pl.BlockSpec((tm, a_w), lambda i: (i, 0)),
        out_shape=jax.ShapeDtypeStruct((n, a_w), _BF16),
        compiler_params=_params(("parallel",)),
        name="gmlp",
    )(uv, g_v, w_s, b_s)


def _gla_kernel(qf_ref, kf_ref, vf_ref, cf_ref, qb_ref, kb_ref, vb_ref, cb_ref, of_ref, ob_ref, s_ref):
    @pl.when(pl.program_id(1) == 0)
    def _():
        s_ref[...] = jnp.zeros_like(s_ref)

    group = qf_ref.shape[0] // CHUNK
    streams = ((qf_ref, kf_ref, vf_ref, cf_ref, of_ref), (qb_ref, kb_ref, vb_ref, cb_ref, ob_ref))
    for g in range(group):
        for d, (q_ref, k_ref, v_ref, c_ref, o_ref) in enumerate(streams):
            scanned, _ = _scan_masks(d)
            c = g if d == 0 else group - 1 - g
            rows = slice(c * CHUNK, (c + 1) * CHUNK)
            for h in range(B_HEADS):
                kc = slice(h * B_DK, (h + 1) * B_DK)
                vc = slice(h * B_DV, (h + 1) * B_DV)
                cols = c_ref[c, kc, :]
                e_mid, e_end, decay = (cols[:, 3 * d + r:3 * d + r + 1] for r in range(3))
                qd = q_ref[rows, kc]
                kd_t = k_ref[c, kc, :]
                vh = v_ref[rows, vc]
                state = s_ref[d, h]
                scores = jnp.where(scanned, _dot(qd, kd_t), 0.0).astype(_BF16)
                o_ref[rows, vc] = _dot(scores, vh) + _dot(qd, (e_mid * state).astype(_BF16))
                s_ref[d, h] = decay * state + e_end * _dot(kd_t, vh)


def _gla_call(qf, kf, qb, kb, v, cols, n_batch, ctx_chunks, lat_chunks):
    n = v.shape[0]
    k_w = B_HEADS * B_DK
    v_w = B_HEADS * B_DV
    group = GLA_GROUP
    assert ctx_chunks % group == 0 and lat_chunks % group == 0
    ctx_steps = ctx_chunks // group
    lat_steps = lat_chunks // group
    steps = ctx_steps + lat_steps
    lat0 = n_batch * ctx_steps

    def fwd(b, t):
        return jnp.where(t < ctx_steps, b * ctx_steps + t, lat0 + b * lat_steps + (t - ctx_steps))

    def bwd(b, t):
        return jnp.where(t < ctx_steps, b * ctx_steps + (ctx_steps - 1 - t),
                         lat0 + b * lat_steps + (steps - 1 - t))

    def specs(group_of):
        tok = lambda w: pl.BlockSpec((group * CHUNK, w), lambda b, t: (group_of(b, t), 0))
        per_chunk = pl.BlockSpec((group, k_w, CHUNK), lambda b, t: (group_of(b, t), 0, 0))
        return [tok(k_w), per_chunk, tok(v_w), per_chunk]

    return pl.pallas_call(
        _gla_kernel,
        grid=(n_batch, steps),
        in_specs=specs(fwd) + specs(bwd),
        out_specs=[pl.BlockSpec((group * CHUNK, v_w), lambda b, t: (fwd(b, t), 0)),
                   pl.BlockSpec((group * CHUNK, v_w), lambda b, t: (bwd(b, t), 0))],
        out_shape=[jax.ShapeDtypeStruct((n, v_w), _F32)] * 2,
        scratch_shapes=[pltpu.VMEM((2, B_HEADS, B_DK, B_DV), _F32)],
        compiler_params=_params(("parallel", "arbitrary")),
        name="gla_scan",
    )(qf, kf, v, cols, qb, kb, v, cols)


def _outproj_kernel(*refs, routed, n_stream):
    x_refs, refs = refs[:n_stream], refs[n_stream:]
    if routed:
        (ya_ref, of_ref, ob_ref, gate_ref, mt_ref, gn_ref, gpost_ref, gffn_ref, w_ref, wr_ref,
         xo_ref, f_ref, route_ref, y_scr, f_scr) = refs
    else:
        (ya_ref, of_ref, ob_ref, gate_ref, mt_ref, gn_ref, gpost_ref, gffn_ref, w_ref,
         xo_ref, f_ref, y_scr) = refs
    a_w = A_HEADS * A_HEAD_DIM
    tm = x_refs[0].shape[0]
    for s in range(tm // MOD_BLOCK):
        rows = slice(s * MOD_BLOCK, (s + 1) * MOD_BLOCK)
        y_scr[rows, :a_w] = ya_ref[rows, :]
        for h in range(B_HEADS):
            vc = slice(h * B_DV, (h + 1) * B_DV)
            o = of_ref[rows, vc] + ob_ref[rows, vc]
            o = o * lax.rsqrt(jnp.mean(o * o, axis=-1, keepdims=True) + EPS) * gn_ref[:, vc]
            y_scr[rows, a_w + h * B_DV:a_w + (h + 1) * B_DV] = (
                o * _silu(gate_ref[rows, vc].astype(_F32))).astype(_BF16)
        m = _rms(_dot(y_scr[rows, :], w_ref[...]), gpost_ref[...])
        mt = mt_ref[s]
        x = _stream_rows(x_refs, rows) + mt[2:3] * m
        xo_ref[rows, :] = x
        f = _rms(x, gffn_ref[...]) * (1.0 + mt[4:5]) + mt[3:4]
        if routed:
            f_scr[rows, :] = f
            half = f.shape[1] // 2
            f_ref[rows, :] = _pack_bf16_pair(f[:, :half], f[:, half:])
        else:
            f_ref[rows, :] = f.astype(_BF16)
    if routed:
        logits = _dot_f32(wr_ref[...], f_scr[...], _dot_nt)
        eidx = lax.broadcasted_iota(jnp.int32, logits.shape, 0).astype(_F32)
        none = float(N_EXPERTS)
        m1 = jnp.max(logits, axis=0, keepdims=True)
        i1 = jnp.min(jnp.where(logits == m1, eidx, none), axis=0, keepdims=True)
        rest = jnp.where(eidx == i1, -jnp.inf, logits)
        m2 = jnp.max(rest, axis=0, keepdims=True)
        i2 = jnp.min(jnp.where(rest == m2, eidx, none), axis=0, keepdims=True)
        e2 = jnp.exp(m2 - m1)
        w1 = 1.0 / (1.0 + e2)
        w2 = e2 / (1.0 + e2)
        r = lax.broadcasted_iota(jnp.int32, logits.shape, 0)
        route_ref[...] = jnp.where(r == 0, i1, jnp.where(r == 1, i2, jnp.where(r == 2, w1, jnp.where(r == 3, w2, 0.0))))


def _outproj_call(layer, ya, o_f, o_b, gate, x, modtab, g_norm, g_post, g_ffn, w_out, wr_t, route_layer):
    n, d = _stream_shape(x)
    tm = TOKEN_TILE
    routed = wr_t is not None
    tok = lambda w: pl.BlockSpec((tm, w), lambda i: (i, 0))
    a_w = A_HEADS * A_HEAD_DIM
    v_w = B_HEADS * B_DV
    stream_specs = _stream_specs(x, tm)
    in_specs = stream_specs + [
                tok(a_w), tok(v_w), tok(v_w), tok(v_w),
                _layer_block(modtab, layer, (tm // MOD_BLOCK, 8, d), lambda i: (i, 0, 0)),
                _layer_block(g_norm, layer), _layer_block(g_post, layer), _layer_block(g_ffn, layer),
                _resident(w_out, layer)]
    args = _stream_args(x) + [ya, o_f, o_b, gate, modtab, g_norm, g_post, g_ffn, w_out]
    scratch = [pltpu.VMEM((tm, a_w + v_w), _BF16)]
    out_specs = [tok(d), tok(d // 2 if routed else d)]
    out_shape = [jax.ShapeDtypeStruct((n, d), _F32),
                 jax.ShapeDtypeStruct((n, d // 2), jnp.uint32) if routed else jax.ShapeDtypeStruct((n, d), _BF16)]
    if routed:
        scratch.append(pltpu.VMEM((tm, d), _F32))
        in_specs.append(_layer_block(wr_t, route_layer))
        args.append(wr_t)
        out_specs.append(pl.BlockSpec((ROUTE_ROWS, tm), lambda i: (0, i)))
        out_shape.append(jax.ShapeDtypeStruct((ROUTE_ROWS, n), _F32))
    return pl.pallas_call(
        functools.partial(_outproj_kernel, routed=routed, n_stream=len(stream_specs)),
        grid=(n // tm,),
        in_specs=in_specs,
        out_specs=out_specs,
        out_shape=out_shape,
        scratch_shapes=scratch,
        compiler_params=_params(("parallel",)),
        name="out_proj_routed" if routed else "out_proj",
    )(*args)


def _ffn_kernel(f_ref, wg_ref, wu_ref, wd_ref, x_ref, mt_ref, g_ref, o_ref):
    j = pl.program_id(1)

    @pl.when(j == 0)
    def _():
        o_ref[...] = jnp.zeros_like(o_ref)

    f = f_ref[...]
    act = (_silu(_dot(f, wg_ref[...].astype(_BF16))) * _dot(f, wu_ref[...].astype(_BF16))).astype(_BF16)
    o_ref[...] += _dot(act, wd_ref[...].astype(_BF16))

    @pl.when(j == pl.num_programs(1) - 1)
    def _():
        for s in range(o_ref.shape[0] // MOD_BLOCK):
            rows = slice(s * MOD_BLOCK, (s + 1) * MOD_BLOCK)
            o_ref[rows, :] = x_ref[rows, :] + mt_ref[s][5:6] * _rms(o_ref[rows, :], g_ref[...])


def _token_tile(n, target):
    tm = target - target % MOD_BLOCK
    while n % tm:
        tm -= MOD_BLOCK
    return tm


def _ffn_call(layer, ffn_layer, f, w_gate, w_up, w_down, x, modtab, g_post):
    n, d = x.shape
    d_ff = w_gate.shape[2]
    tm = _token_tile(n, FFN_TOKEN_TILE)
    tf = FF_TILE
    return pl.pallas_call(
        _ffn_kernel,
        grid=(n // tm, d_ff // tf),
        in_specs=[
            pl.BlockSpec((tm, d), lambda i, j: (i, 0)),
            _layer_block(w_gate, ffn_layer, (d, tf), lambda i, j: (0, j)),
            _layer_block(w_up, ffn_layer, (d, tf), lambda i, j: (0, j)),
            _layer_block(w_down, ffn_layer, (tf, d), lambda i, j: (j, 0)),
            pl.BlockSpec((tm, d), lambda i, j: (i, 0)),
            _layer_block(modtab, layer, (tm // MOD_BLOCK, 8, d), lambda i, j: (i, 0, 0)),
            _layer_block(g_post, layer),
        ],
        out_specs=pl.BlockSpec((tm, d), lambda i, j: (i, 0)),
        out_shape=jax.ShapeDtypeStruct((n, d), _F32),
        compiler_params=_params(("parallel", "arbitrary")),
        name="dense_ffn",
    )(f, w_gate, w_up, w_down, x, modtab, g_post)


def _rank_kernel(route_ref, rank_ref, count_ref, carry):
    @pl.when(pl.program_id(0) == 0)
    def _():
        carry[...] = jnp.zeros_like(carry)

    tl = route_ref.shape[1]
    r = route_ref[...]
    eidx = lax.broadcasted_iota(jnp.int32, r.shape, 0).astype(_F32)
    oh1 = jnp.where(eidx == r[0:1], 1.0, 0.0)
    oh2 = jnp.where(eidx == r[1:2], 1.0, 0.0)
    both = oh1 + oh2
    before = lax.broadcasted_iota(jnp.int32, (tl, tl), 0) < lax.broadcasted_iota(jnp.int32, (tl, tl), 1)
    seen = carry[:, 0:1] + _dot(both.astype(_BF16), jnp.where(before, 1.0, 0.0).astype(_BF16))
    rank1 = jnp.sum(oh1 * seen, axis=0, keepdims=True)
    rank2 = jnp.sum(oh2 * seen, axis=0, keepdims=True)
    row = lax.broadcasted_iota(jnp.int32, r.shape, 0)
    rank_ref[...] = jnp.where(row == 0, rank1, jnp.where(row == 1, rank2, 0.0))
    carry[...] = carry[...] + jnp.sum(both, axis=1, keepdims=True)
    count_ref[...] = carry[...]


def _rank_call(route):
    n = route.shape[1]
    tl = TOKEN_TILE
    return pl.pallas_call(
        _rank_kernel,
        grid=(n // tl,),
        in_specs=[pl.BlockSpec((ROUTE_ROWS, tl), lambda i: (0, i))],
        out_specs=[pl.BlockSpec((ROUTE_ROWS, tl), lambda i: (0, i)),
                   pl.BlockSpec((ROUTE_ROWS, 128), lambda i: (0, 0))],
        out_shape=[jax.ShapeDtypeStruct((ROUTE_ROWS, n), _F32),
                   jax.ShapeDtypeStruct((ROUTE_ROWS, 128), _F32)],
        scratch_shapes=[pltpu.VMEM((ROUTE_ROWS, 128), _F32)],
        compiler_params=_params(("arbitrary",)),
        name="moe_rank",
    )(route)


def _dispatch_kernel(dest_ref, f_ref, xs_in_ref, xs_ref, sem):
    del xs_in_ref
    td = f_ref.shape[0]

    def row_copy(t, k):
        return pltpu.make_async_copy(f_ref.at[pl.ds(t, 1)], xs_ref.at[pl.ds(dest_ref[0, k * td + t], 1)], sem)

    def issue(t, carry):
        row_copy(t, 0).start()
        row_copy(t, 1).start()
        return carry

    lax.fori_loop(0, td, issue, 0, unroll=DMA_ISSUE_UNROLL)
    for _ in range(2):
        pltpu.make_async_copy(f_ref, xs_ref.at[pl.ds(0, td)], sem).wait()


def _dispatch_call(dest_blocks, f, xs_init):
    n_steps = dest_blocks.shape[0]
    return pl.pallas_call(
        _dispatch_kernel,
        grid=(n_steps,),
        in_specs=[
            pl.BlockSpec((None, 1, dest_blocks.shape[2]), lambda i: (i, 0, 0), memory_space=pltpu.SMEM),
            pl.BlockSpec((dest_blocks.shape[2] // 2, f.shape[1]), lambda i: (i, 0)),
            pl.BlockSpec(memory_space=pl.ANY),
        ],
        out_specs=pl.BlockSpec(memory_space=pl.ANY),
        out_shape=jax.ShapeDtypeStruct(xs_init.shape, xs_init.dtype),
        scratch_shapes=[pltpu.SemaphoreType.DMA(())],
        input_output_aliases={2: 0},
        compiler_params=_params(("arbitrary",)),
        name="moe_dispatch",
    )(dest_blocks, f, xs_init)


def _expert_kernel(be_ref, rows_ref, xs_ref, wg_ref, wu_ref, wd_ref, ys_ref, xb_scr):
    del be_ref
    n_rows = rows_ref[pl.program_id(0)]

    @pl.when(pl.program_id(1) == 0)
    def _():
        ys_ref[...] = jnp.zeros_like(ys_ref)
        half = xs_ref.shape[1]
        xb_scr[:, :half], xb_scr[:, half:] = _unpack_bf16_pair(xs_ref[...])

    def swiglu_pieces(*pieces):
        wg, wu, wd = (ref[...].astype(_BF16) for ref in (wg_ref, wu_ref, wd_ref))
        for rows in pieces:
            xb = xb_scr[rows, :]
            act = (_silu(_dot(xb, wg)) * _dot(xb, wu)).astype(_BF16)
            ys_ref[rows, :] += _dot(act, wd)

    halves = [slice(b, b + EXPERT_HALF_ROWS) for b in range(0, EXPERT_ROWS, EXPERT_HALF_ROWS)]
    pl.when(n_rows == EXPERT_ROWS)(functools.partial(swiglu_pieces, *halves))
    for rows in halves:
        base = rows.start
        pl.when((n_rows >= base + EXPERT_HALF_ROWS) & (n_rows < EXPERT_ROWS))(functools.partial(swiglu_pieces, rows))
        for part in range(EXPERT_SUB_ROWS, EXPERT_HALF_ROWS, EXPERT_SUB_ROWS):
            pl.when(n_rows == base + part)(functools.partial(swiglu_pieces, slice(base, base + part)))


def _expert_call(moe_layer, block_expert, block_rows, xs, w_gate, w_up, w_down):
    cap = xs.shape[0]
    d, d_ff = w_gate.shape[2:]
    tb = EXPERT_ROWS
    tf = FF_TILE
    n_ff = d_ff // tf

    def ff_of(i, j, rows):
        return jnp.where(rows[i] > 0, j, n_ff - 1)

    grid_spec = pltpu.PrefetchScalarGridSpec(
        num_scalar_prefetch=2,
        grid=(cap // tb, n_ff),
        in_specs=[
            pl.BlockSpec((tb, xs.shape[1]), lambda i, j, be, rows: (i, 0)),
            _layer_block(w_gate, moe_layer, (None, d, tf), lambda i, j, be, rows: (be[i], 0, ff_of(i, j, rows))),
            _layer_block(w_up, moe_layer, (None, d, tf), lambda i, j, be, rows: (be[i], 0, ff_of(i, j, rows))),
            _layer_block(w_down, moe_layer, (None, tf, d), lambda i, j, be, rows: (be[i], ff_of(i, j, rows), 0)),
        ],
        out_specs=pl.BlockSpec((tb, d), lambda i, j, be, rows: (i, 0)),
        scratch_shapes=[pltpu.VMEM((tb, d), _BF16)],
    )
    return pl.pallas_call(
        _expert_kernel,
        grid_spec=grid_spec,
        out_shape=jax.ShapeDtypeStruct((cap, d), _F32),
        compiler_params=_params(("parallel", "arbitrary"), EXPERT_VMEM_LIMIT),
        name="moe_experts",
    )(block_expert, block_rows, xs, w_gate, w_up, w_down)


def _combine_kernel(dest_ref, dest_next_ref, ys_ref, w_ref, x_ref, mt_ref, g_ref, o_ref, buf, sems, *, skip_blocks):
    tc = x_ref.shape[0]
    i = pl.program_id(0)
    cur = i % 2

    def gather(idx_ref, b):
        def row_copy(t, k):
            return pltpu.make_async_copy(ys_ref.at[pl.ds(idx_ref[0, k * tc + t], 1)], buf.at[b, k, pl.ds(t, 1)], sems.at[b])

        def issue(t, carry):
            row_copy(t, 0).start()
            row_copy(t, 1).start()
            return carry

        lax.fori_loop(0, tc, issue, 0, unroll=DMA_ISSUE_UNROLL)

    pl.when(i == skip_blocks)(lambda: gather(dest_ref, cur))
    pl.when((i >= skip_blocks) & (i < pl.num_programs(0) - 1))(lambda: gather(dest_next_ref, 1 - cur))

    @pl.when(i >= skip_blocks)
    def _():
        for k in range(2):
            pltpu.make_async_copy(ys_ref.at[pl.ds(0, tc)], buf.at[cur, k], sems.at[cur]).wait()
        w = w_ref[...]
        f = w[:, 0:1] * buf[cur, 0] + w[:, 1:2] * buf[cur, 1]
        for s in range(tc // MOD_BLOCK):
            rows = slice(s * MOD_BLOCK, (s + 1) * MOD_BLOCK)
            o_ref[rows, :] = x_ref[rows, :] + mt_ref[s][5:6] * _rms(f[rows, :], g_ref[...])


def _combine_call(layer, dest_blocks, ys, w_cols, x, modtab, g_post, skip_rows):
    n, d = x.shape
    tc = COMBINE_ROWS
    skip_blocks = skip_rows // tc
    return pl.pallas_call(
        functools.partial(_combine_kernel, skip_blocks=skip_blocks),
        grid=(n // tc,),
        in_specs=[
            pl.BlockSpec((None, 1, 2 * tc), lambda i: (i, 0, 0), memory_space=pltpu.SMEM),
            pl.BlockSpec((None, 1, 2 * tc), lambda i: (jnp.minimum(i + 1, n // tc - 1), 0, 0),
                         memory_space=pltpu.SMEM),
            pl.BlockSpec(memory_space=pl.ANY),
            pl.BlockSpec((tc, 2), lambda i: (i, 0)),
            pl.BlockSpec((tc, d), lambda i: (i, 0)),
            _layer_block(modtab, layer, (tc // MOD_BLOCK, 8, d), lambda i: (i, 0, 0)),
            _layer_block(g_post, layer),
        ],
        out_specs=pl.BlockSpec((tc, d), lambda i: (jnp.maximum(i - skip_blocks, 0), 0)),
        out_shape=jax.ShapeDtypeStruct((n - skip_rows, d), _F32),
        scratch_shapes=[pltpu.VMEM((2, 2, tc, d), _F32), pltpu.SemaphoreType.DMA((2,))],
        compiler_params=_params(("arbitrary",)),
        name="moe_combine",
    )(dest_blocks, dest_blocks, ys, w_cols, x, modtab, g_post)


def _dest_blocks(dest, rows):
    n = dest.shape[1]
    return dest.reshape(2, n // rows, rows).transpose(1, 0, 2).reshape(n // rows, 1, 2 * rows)


def _moe_call(layer, moe_layer, f, route, x, modtab, g_post, w_gate, w_up, w_down, skip_rows):
    n, d = x.shape
    tb = EXPERT_ROWS
    rank, counts = _rank_call(route)
    expert = route[0:2].astype(jnp.int32)
    counts = counts[:, 0].astype(jnp.int32)
    padded = (counts + tb - 1) // tb * tb
    pad_end = jnp.cumsum(padded)
    pad_start = pad_end - padded
    dest = rank[0:2].astype(jnp.int32) + sum(jnp.where(expert == e, pad_start[e], 0) for e in range(N_EXPERTS))
    n_blocks = -(-(2 * n + N_EXPERTS * (tb - 1)) // tb)
    block_start = jnp.arange(n_blocks, dtype=jnp.int32) * tb
    block_expert = jnp.minimum(jnp.sum(block_start[:, None] >= pad_end[None, :], axis=1), N_EXPERTS - 1).astype(jnp.int32)
    sub = EXPERT_SUB_ROWS
    used_end = pad_start + (counts + sub - 1) // sub * sub
    block_used_end = sum(jnp.where(block_expert == e, used_end[e], 0) for e in range(N_EXPERTS))
    block_rows = jnp.clip(block_used_end - block_start, 0, tb).astype(jnp.int32)
    xs = _dispatch_call(_dest_blocks(dest, TOKEN_TILE), f, jnp.zeros((n_blocks * tb, f.shape[1]), f.dtype))
    ys = _expert_call(moe_layer, block_expert, block_rows, xs, w_gate, w_up, w_down)
    return _combine_call(layer, _dest_blocks(dest, COMBINE_ROWS), ys, route[2:4].T, x, modtab, g_post, skip_rows)


def kernel(x, c, ctx, c_ctx, w_mod, b_mod, g_pre_mix, g_post_mix, g_pre_ffn, g_post_ffn, w_in, w_out, g_vnorm, w_spatial, b_spatial, w_decay, b_decay, g_gla_norm, w_ffn_gate, w_ffn_up, w_ffn_down, w_router, w_exp_gate, w_exp_up, w_exp_down):
    n_batch, seq, d = x.shape
    ctx_len = ctx.shape[1]
    depth = w_mod.shape[0]
    a_w = A_HEADS * A_HEAD_DIM
    k_w = B_HEADS * B_DK
    v_w = B_HEADS * B_DV
    main_w = 2 * a_w + 2 * k_w + 2 * v_w
    assert w_in.shape[2] == main_w + 2 * DECAY_RANK and seq % TOKEN_TILE == 0
    assert (n_batch * ctx_len) % TOKEN_TILE == 0 and ctx_len % MOD_BLOCK == 0

    stream = (ctx.reshape(n_batch * ctx_len, d), x.reshape(n_batch * seq, d))

    cvec = jnp.zeros((ROUTE_ROWS, d), _F32).at[:n_batch].set(c).at[n_batch].set(c_ctx)
    mod = _mod_call(cvec, w_mod, b_mod).reshape(depth, ROUTE_ROWS, 6, d)
    per_block = lambda r, blocks: jnp.broadcast_to(mod[:, r:r + 1], (depth, blocks, 6, d))
    modtab = jnp.concatenate([per_block(n_batch, n_batch * ctx_len // MOD_BLOCK)]
                             + [per_block(b, seq // MOD_BLOCK) for b in range(n_batch)], axis=1)
    modtab = jnp.pad(modtab, ((0, 0), (0, 0), (0, 2), (0, 0)))

    w_main = w_in.astype(_BF16)
    w_a = jnp.pad(w_in[:, :, main_w:].astype(_BF16), ((0, 0), (0, 0), (0, CHUNK - 2 * DECAY_RANK)))
    wd = jnp.stack([jnp.pad(w_decay[:, s], ((0, 0), (s * DECAY_RANK, CHUNK - (s + 1) * DECAY_RANK), (0, 0)))
                    for s in range(2)], axis=1)
    bd = b_decay[:, :, None, :]
    w_out_b = w_out.astype(_BF16)
    w_s = w_spatial.astype(_BF16)
    b_s = b_spatial[:, :, :, None]
    wr_t = w_router.transpose(0, 2, 1)
    rows = lambda g: g[:, None, :]
    g_pre_mix, g_post_mix, g_pre_ffn, g_post_ffn, g_vnorm, g_gla_norm = map(
        rows, (g_pre_mix, g_post_mix, g_pre_ffn, g_post_ffn, g_vnorm, g_gla_norm))

    n_ctx = n_batch * ctx_len
    for l in range(depth):
        last = l == depth - 1
        uv, gate, v, qf, kf, qb, kb, cols = _inproj_call(l, stream, modtab, g_pre_mix, w_main, w_a, wd, bd)
        ya = _gmlp_call(l, uv, g_vnorm, w_s, b_s)
        o_f, o_b = _gla_call(qf, kf, qb, kb, v, cols, n_batch, ctx_len // CHUNK, seq // CHUNK)
        routed = l % 2 == 1
        outs = _outproj_call(l, ya, o_f, o_b, gate, stream, modtab, g_gla_norm, g_post_mix, g_pre_ffn, w_out_b,
                             wr_t if routed else None, l // 2)
        if routed:
            stream, f, route = outs
            stream = _moe_call(l, l // 2, f, route, stream, modtab, g_post_ffn, w_exp_gate, w_exp_up, w_exp_down,
                               n_ctx if last else 0)
        else:
            stream, f = outs
            stream = _ffn_call(l, l // 2, f, w_ffn_gate, w_ffn_up, w_ffn_down, stream, modtab, g_post_ffn)
            if last:
                stream = stream[n_ctx:]
    return stream.reshape(n_batch, seq, d)
```

```python
import functools

import jax
import jax.numpy as jnp
from jax import lax
from jax.experimental import pallas as pl
from jax.experimental.pallas import tpu as pltpu

_F32 = jnp.float32
_BF16 = jnp.bfloat16

EPS = 1e-6
CHUNK = 128
GLA_GROUP = 2
MOD_BLOCK = 256
A_HEADS = 8
A_HEAD_DIM = 128
B_HEADS = 4
B_DK = 128
B_DV = 256
DECAY_RANK = 16
DECAY_TAU = 16.0
N_EXPERTS = 8
ROUTE_ROWS = 8
TOKEN_TILE = 512
FFN_TOKEN_TILE = 768
FF_TILE = 256
EXPERT_ROWS = 1536
EXPERT_HALF_ROWS = 512
EXPERT_SUB_ROWS = 256
COMBINE_ROWS = 256
DMA_ISSUE_UNROLL = 8
VMEM_LIMIT = 56 << 20
EXPERT_VMEM_LIMIT = 60 << 20


def _dot(a, b):
    return jnp.dot(a, b, preferred_element_type=_F32)


def _dot_nt(a, b):
    return lax.dot_general(a, b, (((1,), (1,)), ((), ())), preferred_element_type=_F32)


def _dot_tn(a, b):
    return lax.dot_general(a, b, (((0,), (0,)), ((), ())), preferred_element_type=_F32)


def _split(x):
    hi = x.astype(_BF16)
    lo = (x - hi.astype(_F32)).astype(_BF16)
    return hi, lo


def _dot_f32(a, b, dot=_dot):
    ah, al = _split(a)
    bh, bl = _split(b)
    return dot(ah, bh) + dot(ah, bl) + dot(al, bh)


def _rms(x, g):
    return x * lax.rsqrt(jnp.mean(x * x, axis=-1, keepdims=True) + EPS) * g


def _gelu(x):
    return 0.5 * x * (1.0 + lax.erf(x * (2.0 ** -0.5)))


def _silu(x):
    return x * jax.nn.sigmoid(x)


def _log_sigmoid(z):
    return jnp.minimum(z, 0.0) - jnp.log1p(jnp.exp(-jnp.abs(z)))


def _pack_bf16_pair(left, right):
    hi = lax.bitcast_convert_type(left.astype(_BF16).astype(_F32), jnp.uint32)
    lo = lax.bitcast_convert_type(right.astype(_BF16).astype(_F32), jnp.uint32)
    return hi | (lo >> 16)


def _unpack_bf16_pair(packed):
    left = lax.bitcast_convert_type(packed & jnp.uint32(0xFFFF0000), _F32).astype(_BF16)
    right = lax.bitcast_convert_type(packed << 16, _F32).astype(_BF16)
    return left, right


def _params(semantics, vmem_limit=VMEM_LIMIT):
    return pltpu.CompilerParams(dimension_semantics=semantics, vmem_limit_bytes=vmem_limit)


def _layer_block(arr, layer, block=None, index=None, **kwargs):
    per_layer = arr.shape[1:]
    block = per_layer if block is None else block
    index = (lambda *_: (0,) * len(per_layer)) if index is None else index
    return pl.BlockSpec((None,) + tuple(block), lambda *a: (layer,) + tuple(index(*a)), **kwargs)


def _stream_specs(stream, tm):
    if not isinstance(stream, tuple):
        return [pl.BlockSpec((tm, stream.shape[1]), lambda i: (i, 0))]
    ctx, lat = stream
    assert ctx.shape[0] == tm and lat.shape[0] % tm == 0
    return [pl.BlockSpec((tm, ctx.shape[1]), lambda i: (0, 0)),
            pl.BlockSpec((tm, lat.shape[1]), lambda i: (jnp.maximum(i - 1, 0), 0))]


def _stream_rows(x_refs, rows):
    if len(x_refs) == 1:
        return x_refs[0][rows, :]
    ctx_ref, lat_ref = x_refs
    return jnp.where(pl.program_id(0) == 0, ctx_ref[rows, :], lat_ref[rows, :])


def _stream_shape(stream):
    if not isinstance(stream, tuple):
        return stream.shape
    return stream[0].shape[0] + stream[1].shape[0], stream[0].shape[1]


def _stream_args(stream):
    return list(stream) if isinstance(stream, tuple) else [stream]


def _resident(arr, layer):
    return _layer_block(arr, layer, pipeline_mode=pl.Buffered(1))


def _mod_kernel(c_ref, w_ref, b_ref, o_ref):
    s = _silu(c_ref[...]).astype(_BF16)
    o_ref[0] = _dot(s, w_ref[0].astype(_BF16)) + b_ref[0]


def _mod_call(cvec, w_mod, b_mod):
    depth, d, n = w_mod.shape
    tn = 1024
    return pl.pallas_call(
        _mod_kernel,
        grid=(depth, n // tn),
        in_specs=[
            pl.BlockSpec((ROUTE_ROWS, d), lambda l, j: (0, 0)),
            pl.BlockSpec((1, d, tn), lambda l, j: (l, 0, j)),
            pl.BlockSpec((1, 1, tn), lambda l, j: (l, 0, j)),
        ],
        out_specs=pl.BlockSpec((1, ROUTE_ROWS, tn), lambda l, j: (l, 0, j)),
        out_shape=jax.ShapeDtypeStruct((depth, ROUTE_ROWS, n), _F32),
        compiler_params=_params(("parallel", "parallel")),
        name="adaln_mod",
    )(cvec, w_mod, b_mod.reshape(depth, 1, n))


def _scan_masks(direction):
    row = lax.broadcasted_iota(jnp.int32, (CHUNK, CHUNK), 0)
    col = lax.broadcasted_iota(jnp.int32, (CHUNK, CHUNK), 1)
    return ((col <= row), CHUNK - 1) if direction == 0 else ((col >= row), 0)


def _inproj_kernel(*refs, n_stream):
    x_refs = refs[:n_stream]
    (mt_ref, g_ref, w_ref, wa_ref, wd_ref, bd_ref,
     uv_ref, gate_ref, v_ref, qf_ref, kf_ref, qb_ref, kb_ref, cols_ref, h_scr) = refs[n_stream:]
    tm = x_refs[0].shape[0]
    a_w = A_HEADS * A_HEAD_DIM
    k_w = B_HEADS * B_DK
    v_w = B_HEADS * B_DV
    for s in range(tm // MOD_BLOCK):
        rows = slice(s * MOD_BLOCK, (s + 1) * MOD_BLOCK)
        mt = mt_ref[s]
        h = _rms(_stream_rows(x_refs, rows), g_ref[...]) * (1.0 + mt[1:2]) + mt[0:1]
        h_scr[rows, :] = h.astype(_BF16)
    h = h_scr[...]
    off_q = 2 * a_w
    off_gate = off_q + k_w
    off_k = off_gate + v_w
    off_v = off_k + k_w
    q = _dot(h, w_ref[:, off_q:off_q + k_w]) * (B_DK ** -0.5)
    k = _dot(h, w_ref[:, off_k:off_k + k_w])
    a = _dot(h, wa_ref[...])
    sub = lax.broadcasted_iota(jnp.int32, (CHUNK, k_w), 0)
    factor_rows = [jnp.zeros((CHUNK, k_w), _F32) for _ in range(tm // CHUNK)]
    for d, (q_ref, k_ref) in enumerate(((qf_ref, kf_ref), (qb_ref, kb_ref))):
        scanned, end = _scan_masks(d)
        tri = jnp.where(scanned, 1.0, 0.0).astype(_BF16)
        la = _log_sigmoid(_dot_f32(a, wd_ref[d]) + bd_ref[d]) * (1.0 / DECAY_TAU)
        la_hi, la_lo = _split(la)
        for c in range(tm // CHUNK):
            rows = slice(c * CHUNK, (c + 1) * CHUNK)
            b = _dot(tri, la_hi[rows]) + _dot(tri, la_lo[rows])
            b_mid = b[CHUNK // 2:CHUNK // 2 + 1]
            b_end = b[end:end + 1]
            q_ref[rows, :] = (q[rows] * jnp.exp(b - b_mid)).astype(_BF16)
            kd = k[rows] * jnp.exp(b_mid - b)
            for hd in range(B_HEADS):
                kc = slice(hd * B_DK, (hd + 1) * B_DK)
                k_ref[c, kc, :] = kd[:, kc].T.astype(_BF16)
            for r, factor in enumerate((jnp.exp(b_mid), jnp.exp(b_end - b_mid), jnp.exp(b_end))):
                factor_rows[c] = jnp.where(sub == 3 * d + r, factor, factor_rows[c])
    for c in range(tm // CHUNK):
        for hd in range(B_HEADS):
            kc = slice(hd * B_DK, (hd + 1) * B_DK)
            cols_ref[c, kc, :] = factor_rows[c][:, kc].T
    for c0 in range(0, 2 * a_w, a_w):
        uv_ref[:, c0:c0 + a_w] = _dot(h, w_ref[:, c0:c0 + a_w]).astype(_BF16)
    gate_ref[...] = _dot(h, w_ref[:, off_gate:off_gate + v_w]).astype(_BF16)
    v_ref[...] = _dot(h, w_ref[:, off_v:off_v + v_w]).astype(_BF16)


def _inproj_call(layer, x, modtab, g_pre, w_main, w_a, wd, bd):
    n, d = _stream_shape(x)
    tm = TOKEN_TILE
    a_w = A_HEADS * A_HEAD_DIM
    k_w = B_HEADS * B_DK
    v_w = B_HEADS * B_DV
    tok = lambda w: pl.BlockSpec((tm, w), lambda i: (i, 0))
    tok_out = lambda w: (tok(w), jax.ShapeDtypeStruct((n, w), _BF16))
    per_chunk = lambda dtype: (pl.BlockSpec((tm // CHUNK, k_w, CHUNK), lambda i: (i, 0, 0)),
                               jax.ShapeDtypeStruct((n // CHUNK, k_w, CHUNK), dtype))
    outs = [tok_out(2 * a_w), tok_out(v_w), tok_out(v_w), tok_out(k_w), per_chunk(_BF16), tok_out(k_w),
            per_chunk(_BF16), per_chunk(_F32)]
    stream_specs = _stream_specs(x, tm)
    return pl.pallas_call(
        functools.partial(_inproj_kernel, n_stream=len(stream_specs)),
        grid=(n // tm,),
        in_specs=stream_specs + [
            _layer_block(modtab, layer, (tm // MOD_BLOCK, 8, d), lambda i: (i, 0, 0)),
            _layer_block(g_pre, layer),
            _resident(w_main, layer),
            _resident(w_a, layer),
            _layer_block(wd, layer),
            _layer_block(bd, layer),
        ],
        out_specs=[spec for spec, _ in outs],
        out_shape=[shape for _, shape in outs],
        scratch_shapes=[pltpu.VMEM((tm, d), _BF16)],
        compiler_params=_params(("parallel",)),
        name="in_proj",
    )(*_stream_args(x), modtab, g_pre, w_main, w_a, wd, bd)


def _gmlp_kernel(uv_ref, gv_ref, ws_ref, bs_ref, y_ref):
    tm = uv_ref.shape[0]
    a_w = A_HEADS * A_HEAD_DIM
    for c in range(tm // CHUNK):
        rows = slice(c * CHUNK, (c + 1) * CHUNK)
        u = _gelu(uv_ref[rows, :a_w].astype(_F32))
        v = _rms(_gelu(uv_ref[rows, a_w:].astype(_F32)), gv_ref[...]).astype(_BF16)
        for h in range(A_HEADS):
            cols = slice(h * A_HEAD_DIM, (h + 1) * A_HEAD_DIM)
            s = _dot(ws_ref[h], v[:, cols]) + bs_ref[h]
            y_ref[rows, cols] = (u[:, cols] * s).astype(_BF16)


def _gmlp_call(layer, uv, g_v, w_s, b_s):
    n = uv.shape[0]
    tm = TOKEN_TILE
    a_w = A_HEADS * A_HEAD_DIM
    return pl.pallas_call(
        _gmlp_kernel,
        grid=(n // tm,),
        in_specs=[
            pl.BlockSpec((tm, 2 * a_w), lambda i: (i, 0)),
            _layer_block(g_v, layer),
            _layer_block(w_s, layer),
            _layer_block(b_s, layer),
        ],
        out_specs=pl.BlockSpec((tm, a_w), lambda i: (i, 0)),
        out_shape=jax.ShapeDtypeStruct((n, a_w), _BF16),
        compiler_params=_params(("parallel",)),
        name="gmlp",
    )(uv, g_v, w_s, b_s)


def _gla_kernel(qf_ref, kf_ref, vf_ref, cf_ref, qb_ref, kb_ref, vb_ref, cb_ref, of_ref, ob_ref, s_ref):
    @pl.when(pl.program_id(1) == 0)
    def _():
        s_ref[...] = jnp.zeros_like(s_ref)

    group = qf_ref.shape[0] // CHUNK
    streams = ((qf_ref, kf_ref, vf_ref, cf_ref, of_ref), (qb_ref, kb_ref, vb_ref, cb_ref, ob_ref))
    for g in range(group):
        for d, (q_ref, k_ref, v_ref, c_ref, o_ref) in enumerate(streams):
            scanned, _ = _scan_masks(d)
            c = g if d == 0 else group - 1 - g
            rows = slice(c * CHUNK, (c + 1) * CHUNK)
            for h in range(B_HEADS):
                kc = slice(h * B_DK, (h + 1) * B_DK)
                vc = slice(h * B_DV, (h + 1) * B_DV)
                cols = c_ref[c, kc, :]
                e_mid, e_end, decay = (cols[:, 3 * d + r:3 * d + r + 1] for r in range(3))
                qd = q_ref[rows, kc]
                kd_t = k_ref[c, kc, :]
                vh = v_ref[rows, vc]
                state = s_ref[d, h]
                scores = jnp.where(scanned, _dot(qd, kd_t), 0.0).astype(_BF16)
                o_ref[rows, vc] = _dot(scores, vh) + _dot(qd, (e_mid * state).astype(_BF16))
                s_ref[d, h] = decay * state + e_end * _dot(kd_t, vh)


def _gla_call(qf, kf, qb, kb, v, cols, n_batch, ctx_chunks, lat_chunks):
    n = v.shape[0]
    k_w = B_HEADS * B_DK
    v_w = B_HEADS * B_DV
    group = GLA_GROUP
    assert ctx_chunks % group == 0 and lat_chunks % group == 0
    ctx_steps = ctx_chunks // group
    lat_steps = lat_chunks // group
    steps = ctx_steps + lat_steps
    lat0 = n_batch * ctx_steps

    def fwd(b, t):
        return jnp.where(t < ctx_steps, b * ctx_steps + t, lat0 + b * lat_steps + (t - ctx_steps))

    def bwd(b, t):
        return jnp.where(t < ctx_steps, b * ctx_steps + (ctx_steps - 1 - t),
                         lat0 + b * lat_steps + (steps - 1 - t))

    def specs(group_of):
        tok = lambda w: pl.BlockSpec((group * CHUNK, w), lambda b, t: (group_of(b, t), 0))
        per_chunk = pl.BlockSpec((group, k_w, CHUNK), lambda b, t: (group_of(b, t), 0, 0))
        return [tok(k_w), per_chunk, tok(v_w), per_chunk]

    return pl.pallas_call(
        _gla_kernel,
        grid=(n_batch, steps),
        in_specs=specs(fwd) + specs(bwd),
        out_specs=[pl.BlockSpec((group * CHUNK, v_w), lambda b, t: (fwd(b, t), 0)),
                   pl.BlockSpec((group * CHUNK, v_w), lambda b, t: (bwd(b, t), 0))],
        out_shape=[jax.ShapeDtypeStruct((n, v_w), _F32)] * 2,
        scratch_shapes=[pltpu.VMEM((2, B_HEADS, B_DK, B_DV), _F32)],
        compiler_params=_params(("parallel", "arbitrary")),
        name="gla_scan",
    )(qf, kf, v, cols, qb, kb, v, cols)


def _outproj_kernel(*refs, routed, n_stream):
    x_refs, refs = refs[:n_stream], refs[n_stream:]
    if routed:
        (ya_ref, of_ref, ob_ref, gate_ref, mt_ref, gn_ref, gpost_ref, gffn_ref, w_ref, wr_ref,
         xo_ref, f_ref, route_ref, y_scr, f_scr) = refs
    else:
        (ya_ref, of_ref, ob_ref, gate_ref, mt_ref, gn_ref, gpost_ref, gffn_ref, w_ref,
         xo_ref, f_ref, y_scr) = refs
    a_w = A_HEADS * A_HEAD_DIM
    tm = x_refs[0].shape[0]
    for s in range(tm // MOD_BLOCK):
        rows = slice(s * MOD_BLOCK, (s + 1) * MOD_BLOCK)
        y_scr[rows, :a_w] = ya_ref[rows, :]
        for h in range(B_HEADS):
            vc = slice(h * B_DV, (h + 1) * B_DV)
            o = of_ref[rows, vc] + ob_ref[rows, vc]
            o = o * lax.rsqrt(jnp.mean(o * o, axis=-1, keepdims=True) + EPS) * gn_ref[:, vc]
            y_scr[rows, a_w + h * B_DV:a_w + (h + 1) * B_DV] = (
                o * _silu(gate_ref[rows, vc].astype(_F32))).astype(_BF16)
        m = _rms(_dot(y_scr[rows, :], w_ref[...]), gpost_ref[...])
        mt = mt_ref[s]
        x = _stream_rows(x_refs, rows) + mt[2:3] * m
        xo_ref[rows, :] = x
        f = _rms(x, gffn_ref[...]) * (1.0 + mt[4:5]) + mt[3:4]
        if routed:
            f_scr[rows, :] = f
            half = f.shape[1] // 2
            f_ref[rows, :] = _pack_bf16_pair(f[:, :half], f[:, half:])
        else:
            f_ref[rows, :] = f.astype(_BF16)
    if routed:
        logits = _dot_f32(wr_ref[...], f_scr[...], _dot_nt)
        eidx = lax.broadcasted_iota(jnp.int32, logits.shape, 0).astype(_F32)
        none = float(N_EXPERTS)
        m1 = jnp.max(logits, axis=0, keepdims=True)
        i1 = jnp.min(jnp.where(logits == m1, eidx, none), axis=0, keepdims=True)
        rest = jnp.where(eidx == i1, -jnp.inf, logits)
        m2 = jnp.max(rest, axis=0, keepdims=True)
        i2 = jnp.min(jnp.where(rest == m2, eidx, none), axis=0, keepdims=True)
        e2 = jnp.exp(m2 - m1)
        w1 = 1.0 / (1.0 + e2)
        w2 = e2 / (1.0 + e2)
        r = lax.broadcasted_iota(jnp.int32, logits.shape, 0)
        route_ref[...] = jnp.where(r == 0, i1, jnp.where(r == 1, i2, jnp.where(r == 2, w1, jnp.where(r == 3, w2, 0.0))))


def _outproj_call(layer, ya, o_f, o_b, gate, x, modtab, g_norm, g_post, g_ffn, w_out, wr_t, route_layer):
    n, d = _stream_shape(x)
    tm = TOKEN_TILE
    routed = wr_t is not None
    tok = lambda w: pl.BlockSpec((tm, w), lambda i: (i, 0))
    a_w = A_HEADS * A_HEAD_DIM
    v_w = B_HEADS * B_DV
    stream_specs = _stream_specs(x, tm)
    in_specs = stream_specs + [
                tok(a_w), tok(v_w), tok(v_w), tok(v_w),
                _layer_block(modtab, layer, (tm // MOD_BLOCK, 8, d), lambda i: (i, 0, 0)),
                _layer_block(g_norm, layer), _layer_block(g_post, layer), _layer_block(g_ffn, layer),
                _resident(w_out, layer)]
    args = _stream_args(x) + [ya, o_f, o_b, gate, modtab, g_norm, g_post, g_ffn, w_out]
    scratch = [pltpu.VMEM((tm, a_w + v_w), _BF16)]
    out_specs = [tok(d), tok(d // 2 if routed else d)]
    out_shape = [jax.ShapeDtypeStruct((n, d), _F32),
                 jax.ShapeDtypeStruct((n, d // 2), jnp.uint32) if routed else jax.ShapeDtypeStruct((n, d), _BF16)]
    if routed:
        scratch.append(pltpu.VMEM((tm, d), _F32))
        in_specs.append(_layer_block(wr_t, route_layer))
        args.append(wr_t)
        out_specs.append(pl.BlockSpec((ROUTE_ROWS, tm), lambda i: (0, i)))
        out_shape.append(jax.ShapeDtypeStruct((ROUTE_ROWS, n), _F32))
    return pl.pallas_call(
        functools.partial(_outproj_kernel, routed=routed, n_stream=len(stream_specs)),
        grid=(n // tm,),
        in_specs=in_specs,
        out_specs=out_specs,
        out_shape=out_shape,
        scratch_shapes=scratch,
        compiler_params=_params(("parallel",)),
        name="out_proj_routed" if routed else "out_proj",
    )(*args)


def _ffn_kernel(f_ref, wg_ref, wu_ref, wd_ref, x_ref, mt_ref, g_ref, o_ref):
    j = pl.program_id(1)

    @pl.when(j == 0)
    def _():
        o_ref[...] = jnp.zeros_like(o_ref)

    f = f_ref[...]
    act = (_silu(_dot(f, wg_ref[...].astype(_BF16))) * _dot(f, wu_ref[...].astype(_BF16))).astype(_BF16)
    o_ref[...] += _dot(act, wd_ref[...].astype(_BF16))

    @pl.when(j == pl.num_programs(1) - 1)
    def _():
        for s in range(o_ref.shape[0] // MOD_BLOCK):
            rows = slice(s * MOD_BLOCK, (s + 1) * MOD_BLOCK)
            o_ref[rows, :] = x_ref[rows, :] + mt_ref[s][5:6] * _rms(o_ref[rows, :], g_ref[...])


def _token_tile(n, target):
    tm = target - target % MOD_BLOCK
    while n % tm:
        tm -= MOD_BLOCK
    return tm


def _ffn_call(layer, ffn_layer, f, w_gate, w_up, w_down, x, modtab, g_post):
    n, d = x.shape
    d_ff = w_gate.shape[2]
    tm = _token_tile(n, FFN_TOKEN_TILE)
    tf = FF_TILE
    return pl.pallas_call(
        _ffn_kernel,
        grid=(n // tm, d_ff // tf),
        in_specs=[
            pl.BlockSpec((tm, d), lambda i, j: (i, 0)),
            _layer_block(w_gate, ffn_layer, (d, tf), lambda i, j: (0, j)),
            _layer_block(w_up, ffn_layer, (d, tf), lambda i, j: (0, j)),
            _layer_block(w_down, ffn_layer, (tf, d), lambda i, j: (j, 0)),
            pl.BlockSpec((tm, d), lambda i, j: (i, 0)),
            _layer_block(modtab, layer, (tm // MOD_BLOCK, 8, d), lambda i, j: (i, 0, 0)),
            _layer_block(g_post, layer),
        ],
        out_specs=pl.BlockSpec((tm, d), lambda i, j: (i, 0)),
        out_shape=jax.ShapeDtypeStruct((n, d), _F32),
        compiler_params=_params(("parallel", "arbitrary")),
        name="dense_ffn",
    )(f, w_gate, w_up, w_down, x, modtab, g_post)


def _rank_kernel(route_ref, rank_ref, count_ref, carry):
    @pl.when(pl.program_id(0) == 0)
    def _():
        carry[...] = jnp.zeros_like(carry)

    tl = route_ref.shape[1]
    r = route_ref[...]
    eidx = lax.broadcasted_iota(jnp.int32, r.shape, 0).astype(_F32)
    oh1 = jnp.where(eidx == r[0:1], 1.0, 0.0)
    oh2 = jnp.where(eidx == r[1:2], 1.0, 0.0)
    both = oh1 + oh2
    before = lax.broadcasted_iota(jnp.int32, (tl, tl), 0) < lax.broadcasted_iota(jnp.int32, (tl, tl), 1)
    seen = carry[:, 0:1] + _dot(both.astype(_BF16), jnp.where(before, 1.0, 0.0).astype(_BF16))
    rank1 = jnp.sum(oh1 * seen, axis=0, keepdims=True)
    rank2 = jnp.sum(oh2 * seen, axis=0, keepdims=True)
    row = lax.broadcasted_iota(jnp.int32, r.shape, 0)
    rank_ref[...] = jnp.where(row == 0, rank1, jnp.where(row == 1, rank2, 0.0))
    carry[...] = carry[...] + jnp.sum(both, axis=1, keepdims=True)
    count_ref[...] = carry[...]


def _rank_call(route):
    n = route.shape[1]
    tl = TOKEN_TILE
    return pl.pallas_call(
        _rank_kernel,
        grid=(n // tl,),
        in_specs=[pl.BlockSpec((ROUTE_ROWS, tl), lambda i: (0, i))],
        out_specs=[pl.BlockSpec((ROUTE_ROWS, tl), lambda i: (0, i)),
                   pl.BlockSpec((ROUTE_ROWS, 128), lambda i: (0, 0))],
        out_shape=[jax.ShapeDtypeStruct((ROUTE_ROWS, n), _F32),
                   jax.ShapeDtypeStruct((ROUTE_ROWS, 128), _F32)],
        scratch_shapes=[pltpu.VMEM((ROUTE_ROWS, 128), _F32)],
        compiler_params=_params(("arbitrary",)),
        name="moe_rank",
    )(route)


def _dispatch_kernel(dest_ref, f_ref, xs_in_ref, xs_ref, sem):
    del xs_in_ref
    td = f_ref.shape[0]

    def row_copy(t, k):
        return pltpu.make_async_copy(f_ref.at[pl.ds(t, 1)], xs_ref.at[pl.ds(dest_ref[0, k * td + t], 1)], sem)

    def issue(t, carry):
        row_copy(t, 0).start()
        row_copy(t, 1).start()
        return carry

    lax.fori_loop(0, td, issue, 0, unroll=DMA_ISSUE_UNROLL)
    for _ in range(2):
        pltpu.make_async_copy(f_ref, xs_ref.at[pl.ds(0, td)], sem).wait()


def _dispatch_call(dest_blocks, f, xs_init):
    n_steps = dest_blocks.shape[0]
    return pl.pallas_call(
        _dispatch_kernel,
        grid=(n_steps,),
        in_specs=[
            pl.BlockSpec((None, 1, dest_blocks.shape[2]), lambda i: (i, 0, 0), memory_space=pltpu.SMEM),
            pl.BlockSpec((dest_blocks.shape[2] // 2, f.shape[1]), lambda i: (i, 0)),
            pl.BlockSpec(memory_space=pl.ANY),
        ],
        out_specs=pl.BlockSpec(memory_space=pl.ANY),
        out_shape=jax.ShapeDtypeStruct(xs_init.shape, xs_init.dtype),
        scratch_shapes=[pltpu.SemaphoreType.DMA(())],
        input_output_aliases={2: 0},
        compiler_params=_params(("arbitrary",)),
        name="moe_dispatch",
    )(dest_blocks, f, xs_init)


def _expert_kernel(be_ref, rows_ref, xs_ref, wg_ref, wu_ref, wd_ref, ys_ref, xb_scr):
    del be_ref
    n_rows = rows_ref[pl.program_id(0)]

    @pl.when(pl.program_id(1) == 0)
    def _():
        ys_ref[...] = jnp.zeros_like(ys_ref)
        half = xs_ref.shape[1]
        xb_scr[:, :half], xb_scr[:, half:] = _unpack_bf16_pair(xs_ref[...])

    def swiglu_pieces(*pieces):
        wg, wu, wd = (ref[...].astype(_BF16) for ref in (wg_ref, wu_ref, wd_ref))
        for rows in pieces:
            xb = xb_scr[rows, :]
            act = (_silu(_dot(xb, wg)) * _dot(xb, wu)).astype(_BF16)
            ys_ref[rows, :] += _dot(act, wd)

    for used in range(EXPERT_SUB_ROWS, EXPERT_ROWS + 1, EXPERT_SUB_ROWS):
        pieces = [slice(b, min(b + EXPERT_HALF_ROWS, used)) for b in range(0, used, EXPERT_HALF_ROWS)]
        pl.when(n_rows == used)(functools.partial(swiglu_pieces, *pieces))


def _expert_call(moe_layer, block_expert, block_rows, xs, w_gate, w_up, w_down):
    cap = xs.shape[0]
    d, d_ff = w_gate.shape[2:]
    tb = EXPERT_ROWS
    tf = FF_TILE
    n_ff = d_ff // tf

    def ff_of(i, j, rows):
        return jnp.where(rows[i] > 0, j, n_ff - 1)

    grid_spec = pltpu.PrefetchScalarGridSpec(
        num_scalar_prefetch=2,
        grid=(cap // tb, n_ff),
        in_specs=[
            pl.BlockSpec((tb, xs.shape[1]), lambda i, j, be, rows: (i, 0)),
            _layer_block(w_gate, moe_layer, (None, d, tf), lambda i, j, be, rows: (be[i], 0, ff_of(i, j, rows))),
            _layer_block(w_up, moe_layer, (None, d, tf), lambda i, j, be, rows: (be[i], 0, ff_of(i, j, rows))),
            _layer_block(w_down, moe_layer, (None, tf, d), lambda i, j, be, rows: (be[i], ff_of(i, j, rows), 0)),
        ],
        out_specs=pl.BlockSpec((tb, d), lambda i, j, be, rows: (i, 0)),
        scratch_shapes=[pltpu.VMEM((tb, d), _BF16)],
    )
    return pl.pallas_call(
        _expert_kernel,
        grid_spec=grid_spec,
        out_shape=jax.ShapeDtypeStruct((cap, d), _F32),
        compiler_params=_params(("parallel", "arbitrary"), EXPERT_VMEM_LIMIT),
        name="moe_experts",
    )(block_expert, block_rows, xs, w_gate, w_up, w_down)


def _combine_kernel(dest_ref, dest_next_ref, ys_ref, w_ref, x_ref, mt_ref, g_ref, o_ref, buf, sems, *, skip_blocks):
    tc = x_ref.shape[0]
    i = pl.program_id(0)
    cur = i % 2

    def gather(idx_ref, b):
        def row_copy(t, k):
            return pltpu.make_async_copy(ys_ref.at[pl.ds(idx_ref[0, k * tc + t], 1)], buf.at[b, k, pl.ds(t, 1)], sems.at[b])

        def issue(t, carry):
            row_copy(t, 0).start()
            row_copy(t, 1).start()
            return carry

        lax.fori_loop(0, tc, issue, 0, unroll=DMA_ISSUE_UNROLL)

    pl.when(i == skip_blocks)(lambda: gather(dest_ref, cur))
    pl.when((i >= skip_blocks) & (i < pl.num_programs(0) - 1))(lambda: gather(dest_next_ref, 1 - cur))

    @pl.when(i >= skip_blocks)
    def _():
        for k in range(2):
            pltpu.make_async_copy(ys_ref.at[pl.ds(0, tc)], buf.at[cur, k], sems.at[cur]).wait()
        w = w_ref[...]
        f = w[:, 0:1] * buf[cur, 0] + w[:, 1:2] * buf[cur, 1]
        for s in range(tc // MOD_BLOCK):
            rows = slice(s * MOD_BLOCK, (s + 1) * MOD_BLOCK)
            o_ref[rows, :] = x_ref[rows, :] + mt_ref[s][5:6] * _rms(f[rows, :], g_ref[...])


def _combine_call(layer, dest_blocks, ys, w_cols, x, modtab, g_post, skip_rows):
    n, d = x.shape
    tc = COMBINE_ROWS
    skip_blocks = skip_rows // tc
    return pl.pallas_call(
        functools.partial(_combine_kernel, skip_blocks=skip_blocks),
        grid=(n // tc,),
        in_specs=[
            pl.BlockSpec((None, 1, 2 * tc), lambda i: (i, 0, 0), memory_space=pltpu.SMEM),
            pl.BlockSpec((None, 1, 2 * tc), lambda i: (jnp.minimum(i + 1, n // tc - 1), 0, 0),
                         memory_space=pltpu.SMEM),
            pl.BlockSpec(memory_space=pl.ANY),
            pl.BlockSpec((tc, 2), lambda i: (i, 0)),
            pl.BlockSpec((tc, d), lambda i: (i, 0)),
            _layer_block(modtab, layer, (tc // MOD_BLOCK, 8, d), lambda i: (i, 0, 0)),
            _layer_block(g_post, layer),
        ],
        out_specs=pl.BlockSpec((tc, d), lambda i: (jnp.maximum(i - skip_blocks, 0), 0)),
        out_shape=jax.ShapeDtypeStruct((n - skip_rows, d), _F32),
        scratch_shapes=[pltpu.VMEM((2, 2, tc, d), _F32), pltpu.SemaphoreType.DMA((2,))],
        compiler_params=_params(("arbitrary",)),
        name="moe_combine",
    )(dest_blocks, dest_blocks, ys, w_cols, x, modtab, g_post)


def _dest_blocks(dest, rows):
    n = dest.shape[1]
    return dest.reshape(2, n // rows, rows).transpose(1, 0, 2).reshape(n // rows, 1, 2 * rows)


def _moe_call(layer, moe_layer, f, route, x, modtab, g_post, w_gate, w_up, w_down, skip_rows):
    n, d = x.shape
    tb = EXPERT_ROWS
    rank, counts = _rank_call(route)
    expert = route[0:2].astype(jnp.int32)
    counts = counts[:, 0].astype(jnp.int32)
    padded = (counts + tb - 1) // tb * tb
    pad_end = jnp.cumsum(padded)
    pad_start = pad_end - padded
    dest = rank[0:2].astype(jnp.int32) + sum(jnp.where(expert == e, pad_start[e], 0) for e in range(N_EXPERTS))
    n_blocks = -(-(2 * n + N_EXPERTS * (tb - 1)) // tb)
    block_start = jnp.arange(n_blocks, dtype=jnp.int32) * tb
    block_expert = jnp.minimum(jnp.sum(block_start[:, None] >= pad_end[None, :], axis=1), N_EXPERTS - 1).astype(jnp.int32)
    sub = EXPERT_SUB_ROWS
    used_end = pad_start + (counts + sub - 1) // sub * sub
    block_used_end = sum(jnp.where(block_expert == e, used_end[e], 0) for e in range(N_EXPERTS))
    block_rows = jnp.clip(block_used_end - block_start, 0, tb).astype(jnp.int32)
    xs = _dispatch_call(_dest_blocks(dest, TOKEN_TILE), f, jnp.zeros((n_blocks * tb, f.shape[1]), f.dtype))
    ys = _expert_call(moe_layer, block_expert, block_rows, xs, w_gate, w_up, w_down)
    return _combine_call(layer, _dest_blocks(dest, COMBINE_ROWS), ys, route[2:4].T, x, modtab, g_post, skip_rows)


def kernel(x, c, ctx, c_ctx, w_mod, b_mod, g_pre_mix, g_post_mix, g_pre_ffn, g_post_ffn, w_in, w_out, g_vnorm, w_spatial, b_spatial, w_decay, b_decay, g_gla_norm, w_ffn_gate, w_ffn_up, w_ffn_down, w_router, w_exp_gate, w_exp_up, w_exp_down):
    n_batch, seq, d = x.shape
    ctx_len = ctx.shape[1]
    depth = w_mod.shape[0]
    a_w = A_HEADS * A_HEAD_DIM
    k_w = B_HEADS * B_DK
    v_w = B_HEADS * B_DV
    main_w = 2 * a_w + 2 * k_w + 2 * v_w
    assert w_in.shape[2] == main_w + 2 * DECAY_RANK and seq % TOKEN_TILE == 0
    assert (n_batch * ctx_len) % TOKEN_TILE == 0 and ctx_len % MOD_BLOCK == 0

    stream = (ctx.reshape(n_batch * ctx_len, d), x.reshape(n_batch * seq, d))

    cvec = jnp.zeros((ROUTE_ROWS, d), _F32).at[:n_batch].set(c).at[n_batch].set(c_ctx)
    mod = _mod_call(cvec, w_mod, b_mod).reshape(depth, ROUTE_ROWS, 6, d)
    per_block = lambda r, blocks: jnp.broadcast_to(mod[:, r:r + 1], (depth, blocks, 6, d))
    modtab = jnp.concatenate([per_block(n_batch, n_batch * ctx_len // MOD_BLOCK)]
                             + [per_block(b, seq // MOD_BLOCK) for b in range(n_batch)], axis=1)
    modtab = jnp.pad(modtab, ((0, 0), (0, 0), (0, 2), (0, 0)))

    w_main = w_in.astype(_BF16)
    w_a = jnp.pad(w_in[:, :, main_w:].astype(_BF16), ((0, 0), (0, 0), (0, CHUNK - 2 * DECAY_RANK)))
    wd = jnp.stack([jnp.pad(w_decay[:, s], ((0, 0), (s * DECAY_RANK, CHUNK - (s + 1) * DECAY_RANK), (0, 0)))
                    for s in range(2)], axis=1)
    bd = b_decay[:, :, None, :]
    w_out_b = w_out.astype(_BF16)
    w_s = w_spatial.astype(_BF16)
    b_s = b_spatial[:, :, :, None]
    wr_t = w_router.transpose(0, 2, 1)
    rows = lambda g: g[:, None, :]
    g_pre_mix, g_post_mix, g_pre_ffn, g_post_ffn, g_vnorm, g_gla_norm = map(
        rows, (g_pre_mix, g_post_mix, g_pre_ffn, g_post_ffn, g_vnorm, g_gla_norm))

    n_ctx = n_batch * ctx_len
    for l in range(depth):
        last = l == depth - 1
        uv, gate, v, qf, kf, qb, kb, cols = _inproj_call(l, stream, modtab, g_pre_mix, w_main, w_a, wd, bd)
        ya = _gmlp_call(l, uv, g_vnorm, w_s, b_s)
        o_f, o_b = _gla_call(qf, kf, qb, kb, v, cols, n_batch, ctx_len // CHUNK, seq // CHUNK)
        routed = l % 2 == 1
        outs = _outproj_call(l, ya, o_f, o_b, gate, stream, modtab, g_gla_norm, g_post_mix, g_pre_ffn, w_out_b,
                             wr_t if routed else None, l // 2)
        if routed:
            stream, f, route = outs
            stream = _moe_call(l, l // 2, f, route, stream, modtab, g_post_ffn, w_exp_gate, w_exp_up, w_exp_down,
                               n_ctx if last else 0)
        else:
            stream, f = outs
            stream = _ffn_call(l, l // 2, f, w_ffn_gate, w_ffn_up, w_ffn_down, stream, modtab, g_post_ffn)
            if last:
                stream = stream[n_ctx:]
    return stream.reshape(n_batch, seq, d)
```
